```python
import math
import jax
import jax.numpy as jnp
from jax import lax
import numpy as np

D_MODEL = 2048
BATCH = 4
SEQ = 4096
DEPTH = 1

D_MIX = D_MODEL
D_ATT = D_MIX // 2
ATT_HEAD_DIM = 128
ATT_HEADS = D_ATT // ATT_HEAD_DIM
ROT_DIM = ATT_HEAD_DIM // 4
ROPE_THETA = 500000.0
DILATED_PATTERNS = ((128, 1), (512, 4), (2048, 16))

D_MLSTM = D_MIX - D_ATT
M_HEADS = 4
M_V_DIM = D_MLSTM // M_HEADS
M_QK_DIM = M_V_DIM // 2
M_CHUNK = 64
N_GATES = 4

SPLITS = (D_ATT, D_ATT, D_ATT, D_ATT,
          M_HEADS * M_QK_DIM, M_HEADS * M_QK_DIM,
          D_MLSTM, D_MLSTM, D_MLSTM,
          N_GATES * M_HEADS)
D_IN = sum(SPLITS)

NORM_EPS = 1e-6
NEG = -1e30

kernel_name = "hybrid_dilated_attn_mlstm_encoder"


def _rmsnorm(x, w):
    xf = x.astype(jnp.float32)
    return xf * lax.rsqrt(jnp.mean(xf * xf, axis=-1, keepdims=True) + NORM_EPS) * w.astype(jnp.float32)


def _rope_partial(x, cos, sin):
    half = ROT_DIM // 2
    x1 = x[..., :half]
    x2 = x[..., half:ROT_DIM]
    c = cos[None, :, None, :]
    s = sin[None, :, None, :]
    rot = jnp.concatenate([x1 * c - x2 * s, x2 * c + x1 * s], axis=-1)
    return jnp.concatenate([rot, x[..., ROT_DIM:]], axis=-1)


def _dilated_band_attention(q, k, v, dil, n_side):
    B, S, H, hd = q.shape
    L = S // dil
    qb = n_side
    nb = -(-L // qb)
    Lp = nb * qb

    def to_res(t):
        return t.reshape(B, L, dil, H, hd).transpose(0, 2, 3, 1, 4)

    qr = jnp.pad(to_res(q), ((0, 0), (0, 0), (0, 0), (0, Lp - L), (0, 0)))
    qr = qr.reshape(B, dil, H, nb, qb, hd)

    def bands(t):
        tp = jnp.pad(t, ((0, 0), (0, 0), (0, 0), (qb, Lp - L + qb), (0, 0)))
        tp = tp.reshape(B, dil, H, nb + 2, qb, hd)
        return jnp.concatenate([tp[:, :, :, :-2], tp[:, :, :, 1:-1], tp[:, :, :, 2:]], axis=4)

    kb = bands(to_res(k))
    vb = bands(to_res(v))
    q_idx = jnp.arange(nb)[:, None] * qb + jnp.arange(qb)[None, :]
    k_idx = jnp.arange(nb)[:, None] * qb - qb + jnp.arange(3 * qb)[None, :]
    rel = k_idx[:, None, :] - q_idx[:, :, None]
    valid = (jnp.abs(rel) <= n_side) & (k_idx[:, None, :] >= 0) & (k_idx[:, None, :] < L)

    scores = jnp.einsum('brhnqc,brhnkc->brhnqk', qr, kb) * (hd ** -0.5)
    scores = jnp.where(valid, scores, NEG)
    m = jnp.max(scores, axis=-1, keepdims=True)
    p = jnp.exp(scores - m)
    l = jnp.sum(p, axis=-1, keepdims=True)
    o = jnp.einsum('brhnqk,brhnkc->brhnqc', p, vb) / l
    lse = (m + jnp.log(l))[..., 0]

    o = o.reshape(B, dil, H, Lp, hd)[:, :, :, :L].transpose(0, 3, 1, 2, 4).reshape(B, S, H, hd)
    lse = lse.reshape(B, dil, H, Lp)[..., :L].transpose(0, 3, 1, 2).reshape(B, S, H)
    return o, lse


def _mlstm_chunkwise(q, k, v, log_i, log_f):
    B, H, S, dk = q.shape
    dv = v.shape[-1]
    nc = S // M_CHUNK

    def chunks(t):
        return jnp.moveaxis(t.reshape(B, H, nc, M_CHUNK, *t.shape[3:]), 2, 0)

    xs = (chunks(q), chunks(k), chunks(v), chunks(log_i), chunks(log_f))
    tril = jnp.tril(jnp.ones((M_CHUNK, M_CHUNK), dtype=bool))

    def step(carry, inp):
        C, n, m = carry
        qc, kc, vc, lic, lfc = inp
        b = jnp.cumsum(lfc, axis=-1)
        D = b[..., :, None] - b[..., None, :] + lic[..., None, :]
        D = jnp.where(tril, D, NEG)
        inter = b + m[..., None]
        m_row = jnp.maximum(inter, jnp.max(D, axis=-1))
        w_intra = jnp.exp(D - m_row[..., None])
        w_inter = jnp.exp(inter - m_row)
        s = jnp.einsum('bhtd,bhsd->bhts', qc, kc) * w_intra
        num = w_inter[..., None] * jnp.einsum('bhtd,bhde->bhte', qc, C) \
            + jnp.einsum('bhts,bhse->bhte', s, vc)
        den = w_inter * jnp.einsum('bhtd,bhd->bht', qc, n) + jnp.sum(s, axis=-1)
        h = num / jnp.maximum(jnp.abs(den), jnp.exp(-m_row))[..., None]
        bL = b[..., -1]
        g = bL[..., None] - b + lic
        m_new = jnp.maximum(bL + m, jnp.max(g, axis=-1))
        decay = jnp.exp(bL + m - m_new)
        wk = jnp.exp(g - m_new[..., None])
        C_new = decay[..., None, None] * C + jnp.einsum('bhs,bhsd,bhse->bhde', wk, kc, vc)
        n_new = decay[..., None] * n + jnp.einsum('bhs,bhsd->bhd', wk, kc)
        return (C_new, n_new, m_new), h

    init = (jnp.zeros((B, H, dk, dv), jnp.float32),
            jnp.zeros((B, H, dk), jnp.float32),
            jnp.full((B, H), NEG, jnp.float32))
    _, hs = lax.scan(step, init, xs)
    return jnp.moveaxis(hs, 0, 2).reshape(B, H, S, dv)


def _hybrid_layer(x, norm_w, w_in, b_gates, q_norm_w, k_norm_w, m_norm_w, w_out, cos, sin):
    B, S, _ = x.shape
    h = _rmsnorm(x, norm_w).astype(x.dtype)
    proj = jnp.einsum('bsd,df->bsf', h, w_in)
    offsets = np.cumsum(SPLITS)[:-1].tolist()
    q_a, k_a, v_a, z_a, q_m, k_m, v_m, o_m, z_m, g = jnp.split(proj, offsets, axis=-1)

    qa = _rope_partial(_rmsnorm(q_a.reshape(B, S, ATT_HEADS, ATT_HEAD_DIM), q_norm_w), cos, sin)
    ka = _rope_partial(_rmsnorm(k_a.reshape(B, S, ATT_HEADS, ATT_HEAD_DIM), k_norm_w), cos, sin)
    va = v_a.reshape(B, S, ATT_HEADS, ATT_HEAD_DIM).astype(jnp.float32)
    outs, lses = [], []
    for window, dil in DILATED_PATTERNS:
        o, lse = _dilated_band_attention(qa, ka, va, dil, window // (2 * dil))
        outs.append(o)
        lses.append(lse)
    mix_w = jax.nn.softmax(jnp.stack(lses, axis=0), axis=0)
    attn = jnp.sum(mix_w[..., None] * jnp.stack(outs, axis=0), axis=0).reshape(B, S, D_ATT)
    attn = attn * jax.nn.silu(z_a.astype(jnp.float32))

    qm = q_m.reshape(B, S, M_HEADS, M_QK_DIM).astype(jnp.float32).transpose(0, 2, 1, 3)
    km = (k_m.reshape(B, S, M_HEADS, M_QK_DIM).astype(jnp.float32) * (M_QK_DIM ** -0.5)).transpose(0, 2, 1, 3)
    vm = v_m.reshape(B, S, M_HEADS, M_V_DIM).astype(jnp.float32).transpose(0, 2, 1, 3)
    gates = (g.astype(jnp.float32) + b_gates.astype(jnp.float32)).reshape(B, S, N_GATES, M_HEADS)
    gates = gates.transpose(2, 0, 3, 1)
    i_fwd, f_fwd, i_bwd, f_bwd = gates[0], gates[1], gates[2], gates[3]
    h_fwd = _mlstm_chunkwise(qm, km, vm, i_fwd, jax.nn.log_sigmoid(f_fwd))
    flip = lambda t: jnp.flip(t, axis=2)
    h_bwd = flip(_mlstm_chunkwise(flip(qm), flip(km), flip(vm), flip(i_bwd),
                                  flip(jax.nn.log_sigmoid(f_bwd))))
    hm = (h_fwd + h_bwd).transpose(0, 2, 1, 3)
    hm = _rmsnorm(hm, m_norm_w.reshape(M_HEADS, M_V_DIM)).reshape(B, S, D_MLSTM)
    hm = jax.nn.sigmoid(o_m.astype(jnp.float32)) * hm * jax.nn.silu(z_m.astype(jnp.float32))

    y = jnp.concatenate([attn, hm], axis=-1).astype(x.dtype)
    return jnp.einsum('bsf,fd->bsd', y, w_out)


def setup_inputs(seed: int = 0) -> dict:
    key = jax.random.key(seed)
    ks = jax.random.split(key, 12)
    f32 = jnp.float32
    x = jax.random.normal(ks[0], (BATCH, SEQ, D_MODEL), f32)
    norm_w = 1.0 + 0.02 * jax.random.normal(ks[1], (DEPTH, D_MODEL), f32)
    w_in = jax.random.normal(ks[2], (DEPTH, D_MODEL, D_IN), f32) * (D_MODEL ** -0.5)
    i_bias = 0.1 * jax.random.normal(ks[3], (DEPTH, 2, M_HEADS), f32)
    f_bias = jnp.linspace(3.0, 6.0, M_HEADS, dtype=f32)[None, None, :] \
        + 0.1 * jax.random.normal(ks[4], (DEPTH, 2, M_HEADS), f32)
    b_gates = jnp.stack([i_bias[:, 0], f_bias[:, 0], i_bias[:, 1], f_bias[:, 1]], axis=1)
    b_gates = b_gates.reshape(DEPTH, N_GATES * M_HEADS)
    q_norm_w = 1.0 + 0.02 * jax.random.normal(ks[5], (DEPTH, ATT_HEAD_DIM), f32)
    k_norm_w = 1.0 + 0.02 * jax.random.normal(ks[6], (DEPTH, ATT_HEAD_DIM), f32)
    m_norm_w = 1.0 + 0.02 * jax.random.normal(ks[7], (DEPTH, D_MLSTM), f32)
    w_out = jax.random.normal(ks[8], (DEPTH, D_MIX, D_MODEL), f32) * (D_MIX ** -0.5)
    return {"x": x, "norm_w": norm_w, "w_in": w_in, "b_gates": b_gates,
            "q_norm_w": q_norm_w, "k_norm_w": k_norm_w, "m_norm_w": m_norm_w,
            "w_out": w_out}


def reference(x, norm_w, w_in, b_gates, q_norm_w, k_norm_w, m_norm_w, w_out):
    S = x.shape[1]
    pos = jnp.arange(S, dtype=jnp.float32)
    inv_freq = ROPE_THETA ** (-jnp.arange(0, ROT_DIM, 2, dtype=jnp.float32) / ROT_DIM)
    ang = pos[:, None] * inv_freq[None, :]
    cos, sin = jnp.cos(ang), jnp.sin(ang)
    for layer in range(DEPTH):
        x = x + _hybrid_layer(x, norm_w[layer], w_in[layer], b_gates[layer],
                              q_norm_w[layer], k_norm_w[layer], m_norm_w[layer],
                              w_out[layer], cos, sin)
    return x
```

```python
import functools
import math

import jax
import jax.numpy as jnp
import numpy as np
from jax import lax
from jax.experimental import pallas as pl
from jax.experimental.pallas import tpu as pltpu

F32 = jnp.float32
BF16 = jnp.bfloat16

D_MODEL = 2048
D_ATT = 1024
ATT_HEADS = 8
HEAD_DIM = 128
ROT_DIM = 32
ROPE_THETA = 500000.0
N_SIDE = 64
M_HEADS = 4
M_QK = 128
M_V = 256
N_GATES = 16
NORM_EPS = 1e-6
NEG = -1e30

LANES = 128
MLSTM_CHUNK = 256
PROJ_TM = 1024
PROJ_TN = 1024
N_MAIN_STEPS = 7
OUT_TM = 512
VMEM_LIMIT = 56 * 1024 * 1024


def _sigmoid(x):
    return 1.0 / (1.0 + jnp.exp(-x))


def _proj_kernel(x_ref, nw_ref, w_ref, wqm_ref, wt_ref, bias_ref, qw_ref, kw_ref,
                 cos_ref, s1_ref, s2_ref,
                 p_ref, kt_ref, gt_ref, xn_ref, acc_ref):
    n = pl.program_id(1)
    tm = x_ref.shape[0]

    @pl.when(n == 0)
    def _():
        x = x_ref[...]
        ms = jnp.mean(x * x, axis=-1, keepdims=True)
        xn_ref[...] = (x * lax.rsqrt(ms + NORM_EPS) * nw_ref[...]).astype(BF16)

    @pl.when(n < N_MAIN_STEPS)
    def _():
        acc_ref[...] = jnp.dot(xn_ref[...], w_ref[...], preferred_element_type=F32)

    def qk_epilogue(w_row):
        cos = cos_ref[...]
        s1 = s1_ref[...]
        s2 = s2_ref[...]
        for h in range(ATT_HEADS):
            sl = slice(h * HEAD_DIM, (h + 1) * HEAD_DIM)
            a = acc_ref[:, sl]
            ms = jnp.mean(a * a, axis=-1, keepdims=True)
            a = a * lax.rsqrt(ms + NORM_EPS) * w_row
            r = (a * cos + pltpu.roll(a, HEAD_DIM - ROT_DIM // 2, 1) * s1
                 + pltpu.roll(a, ROT_DIM // 2, 1) * s2)
            p_ref[:, sl] = r.astype(BF16)

    @pl.when(n == 0)
    def _():
        qk_epilogue(qw_ref[...] * (HEAD_DIM ** -0.5))

    @pl.when(n == 1)
    def _():
        qk_epilogue(kw_ref[...])

    @pl.when((n == 2) | (n == 4))
    def _():
        p_ref[...] = acc_ref[...].astype(BF16)

    @pl.when((n == 3) | (n == 6))
    def _():
        a = acc_ref[...]
        p_ref[...] = (a * _sigmoid(a)).astype(BF16)

    @pl.when(n == 5)
    def _():
        p_ref[...] = _sigmoid(acc_ref[...]).astype(BF16)

    @pl.when(n == N_MAIN_STEPS)
    def _():
        xn = xn_ref[...]
        qm = jnp.dot(xn, wqm_ref[...], preferred_element_type=F32)
        half = M_HEADS * M_QK
        p_ref[:, :half] = qm.astype(BF16)
        p_ref[:, half:] = jnp.zeros((tm, PROJ_TN - half), BF16)
        t = lax.dot_general(wt_ref[...], xn, (((1,), (1,)), ((), ())),
                            preferred_element_type=F32)
        for hh in range(M_HEADS):
            for j in range(tm // MLSTM_CHUNK):
                kt_ref[hh, j * M_QK:(j + 1) * M_QK, :] = (
                    t[hh * M_QK:(hh + 1) * M_QK, j * MLSTM_CHUNK:(j + 1) * MLSTM_CHUNK]
                    * (M_QK ** -0.5))
        g = t[half:half + N_GATES, :]
        for j in range(tm // LANES):
            gt_ref[j * N_GATES:(j + 1) * N_GATES, :] = (
                g[:, j * LANES:(j + 1) * LANES] + bias_ref[...])


def _input_projection(x2, norm_w, w_main, w_qm, w_t, bias_bc, qw, kw, cos_t, s1_t, s2_t, seq):
    n_tok = x2.shape[0]
    tm, tn = PROJ_TM, PROJ_TN
    tiles_per_seq = seq // tm
    const = lambda m, n: (0, 0)
    grid = (n_tok // tm, N_MAIN_STEPS + 1)
    one = pl.Buffered(1)
    return pl.pallas_call(
        _proj_kernel,
        grid=grid,
        in_specs=[
            pl.BlockSpec((tm, D_MODEL), lambda m, n: (m, 0)),
            pl.BlockSpec((1, D_MODEL), const, pipeline_mode=one),
            pl.BlockSpec((D_MODEL, tn), lambda m, n: (0, jnp.minimum(n, N_MAIN_STEPS - 1))),
            pl.BlockSpec(w_qm.shape, const, pipeline_mode=one),
            pl.BlockSpec(w_t.shape, const, pipeline_mode=one),
            pl.BlockSpec(bias_bc.shape, const, pipeline_mode=one),
            pl.BlockSpec((1, HEAD_DIM), const, pipeline_mode=one),
            pl.BlockSpec((1, HEAD_DIM), const, pipeline_mode=one),
            pl.BlockSpec((tm, HEAD_DIM), lambda m, n: (m % tiles_per_seq, 0)),
            pl.BlockSpec((tm, HEAD_DIM), lambda m, n: (m % tiles_per_seq, 0)),
            pl.BlockSpec((tm, HEAD_DIM), lambda m, n: (m % tiles_per_seq, 0)),
        ],
        out_specs=[
            pl.BlockSpec((tm, tn), lambda m, n: (m, n)),
            pl.BlockSpec((M_HEADS, tm // MLSTM_CHUNK * M_QK, MLSTM_CHUNK), lambda m, n: (0, m, 0)),
            pl.BlockSpec((tm // LANES * N_GATES, LANES), lambda m, n: (m, 0)),
        ],
        out_shape=[
            jax.ShapeDtypeStruct((n_tok, (N_MAIN_STEPS + 1) * tn), BF16),
            jax.ShapeDtypeStruct((M_HEADS, n_tok // MLSTM_CHUNK * M_QK, MLSTM_CHUNK), F32),
            jax.ShapeDtypeStruct((n_tok // LANES * N_GATES, LANES), F32),
        ],
        scratch_shapes=[pltpu.VMEM((tm, D_MODEL), BF16), pltpu.VMEM((tm, tn), F32)],
        compiler_params=pltpu.CompilerParams(
            dimension_semantics=("arbitrary", "arbitrary"), vmem_limit_bytes=VMEM_LIMIT),
        name="in_proj",
    )(x2, norm_w, w_main, w_qm, w_t, bias_bc, qw, kw, cos_t, s1_t, s2_t)


ATT_QT = 128
ATT_KT = ATT_QT + 2 * N_SIDE


def _attn_kernel(q_ref, k_ref, v_ref, z_ref, o_ref, qf, kf, vf, acc_s, m_s, l_s):
    seq = q_ref.shape[0]
    qf[...] = q_ref[...].astype(F32)
    kf[...] = k_ref[...].astype(F32)
    vf[...] = v_ref[...].astype(F32)
    ones = jnp.ones((ATT_KT, HEAD_DIM), BF16)

    def rows(start, size, dil):
        return pl.ds(start, size) if dil == 1 else pl.ds(start, size, stride=dil)

    def local(q_rows, k_rows, nq, off):
        q = qf[q_rows, :].astype(BF16)
        k = kf[k_rows, :].astype(BF16)
        v = vf[k_rows, :].astype(BF16)
        s = lax.dot_general(q, k, (((1,), (1,)), ((), ())), preferred_element_type=F32)
        rel = (lax.broadcasted_iota(jnp.int32, (nq, ATT_KT), 1)
               - lax.broadcasted_iota(jnp.int32, (nq, ATT_KT), 0)) + off
        s = jnp.where(jnp.abs(rel) <= N_SIDE, s, NEG)
        m = jnp.max(s, axis=1, keepdims=True)
        p = jnp.exp(s - m).astype(BF16)
        r = jnp.dot(p, jnp.concatenate([v, ones], axis=1), preferred_element_type=F32)
        return m, r[:, HEAD_DIM:], r[:, :HEAD_DIM]

    def merge(m_o, l_o, a_o, m_t, l_t, a_t):
        m_n = jnp.maximum(m_o, m_t)
        e_o = jnp.exp(m_o - m_n)
        e_t = jnp.exp(m_t - m_n)
        return m_n, e_o * l_o + e_t * l_t, e_o * a_o + e_t * a_t

    def tile_geometry(i, dil):
        length = seq // dil
        kl = jnp.clip(ATT_QT * i - N_SIDE, 0, length - ATT_KT)
        return kl, kl - ATT_QT * i

    dil = 16
    nq16 = seq // dil

    def body16(r4, carry):
        for rr in range(4):
            r = r4 * 4 + rr
            rs = rows(r, nq16, dil)
            m, l, a = local(rs, rs, nq16, 0)
            m_s[rs, :] = jnp.broadcast_to(m, (nq16, HEAD_DIM))
            l_s[rs, :] = l
            acc_s[rs, :] = a
        return carry

    lax.fori_loop(0, dil // 4, body16, 0)

    dil = 4

    def body4(i, carry):
        kl, off = tile_geometry(i, dil)
        for r in range(dil):
            qs = rows(r + dil * ATT_QT * i, ATT_QT, dil)
            ks = rows(r + dil * kl, ATT_KT, dil)
            m, l, a = local(qs, ks, ATT_QT, off)
            m_n, l_n, a_n = merge(m_s[qs, :], l_s[qs, :], acc_s[qs, :], m, l, a)
            m_s[qs, :] = m_n
            l_s[qs, :] = l_n
            acc_s[qs, :] = a_n
        return carry

    lax.fori_loop(0, seq // dil // ATT_QT, body4, 0)

    def body1(i4, carry):
        for ii in range(4):
            i = i4 * 4 + ii
            kl, off = tile_geometry(i, 1)
            qs = pl.ds(pl.multiple_of(ATT_QT * i, ATT_QT), ATT_QT)
            ks = pl.ds(kl, ATT_KT)
            m, l, a = local(qs, ks, ATT_QT, off)
            _, l_n, a_n = merge(m_s[qs, :], l_s[qs, :], acc_s[qs, :], m, l, a)
            o_ref[qs, :] = (a_n / l_n * z_ref[qs, :].astype(F32)).astype(BF16)
        return carry

    lax.fori_loop(0, seq // ATT_QT // 4, body1, 0)


def _attention(p, batch, seq):
    blk = lambda col0: pl.BlockSpec((seq, HEAD_DIM), lambda b, h: (b, col0 + h))
    return pl.pallas_call(
        _attn_kernel,
        grid=(batch, ATT_HEADS),
        in_specs=[blk(0), blk(ATT_HEADS), blk(2 * ATT_HEADS), blk(3 * ATT_HEADS)],
        out_specs=pl.BlockSpec((seq, HEAD_DIM), lambda b, h: (b, h)),
        out_shape=jax.ShapeDtypeStruct((batch * seq, D_ATT), BF16),
        scratch_shapes=[pltpu.VMEM((seq, HEAD_DIM), F32) for _ in range(6)],
        compiler_params=pltpu.CompilerParams(
            dimension_semantics=("arbitrary", "arbitrary"), vmem_limit_bytes=VMEM_LIMIT),
        name="dilated_attn",
    )(p, p, p, p)


def _scan_lanes(x, op, identity, reverse):
    width = x.shape[1]
    pos = lax.broadcasted_iota(jnp.int32, x.shape, 1)
    k = 1
    while k < width:
        if reverse:
            shifted = jnp.where(pos < width - k, pltpu.roll(x, width - k, 1), identity)
        else:
            shifted = jnp.where(pos >= k, pltpu.roll(x, k, 1), identity)
        x = op(x, shifted)
        k *= 2
    return x


def _mlstm_kernel(q_ref, kt_ref, v_ref, og_ref, zg_ref, g_ref, nw_ref, y_ref,
                  hbuf, c_ref, u_s, a_s, sig_s, mp_s, al_s):
    head = pl.program_id(1)
    seq = q_ref.shape[0]
    tc = MLSTM_CHUNK
    nchunk = seq // tc
    blocks_per_chunk = tc // LANES

    def gate_rows(g):
        halves = [g_ref[pl.ds(hf * N_GATES + g, nchunk, stride=blocks_per_chunk * N_GATES), :]
                  for hf in range(blocks_per_chunk)]
        return jnp.concatenate(halves, axis=1)

    for d in range(2):
        reverse = d == 1
        li = gate_rows(2 * d * M_HEADS + head)
        fp = gate_rows((2 * d + 1) * M_HEADS + head)
        lf = jnp.minimum(fp, 0.0) - jnp.log1p(jnp.exp(-jnp.abs(fp)))
        b = _scan_lanes(lf, jnp.add, 0.0, reverse)
        u = li - b
        um = _scan_lanes(u, jnp.maximum, NEG, reverse)
        u_s[d] = u
        m_prev = jnp.full((1, tc), NEG, F32)
        order = range(nchunk - 1, -1, -1) if reverse else range(nchunk)
        for c in order:
            u_max = jnp.max(u[c:c + 1, :], axis=1, keepdims=True)
            b_end = jnp.sum(lf[c:c + 1, :], axis=1, keepdims=True)
            a_end = jnp.maximum(m_prev, u_max)
            mp_s[d, c:c + 1, :] = m_prev
            al_s[d, c:c + 1, :] = a_end
            m_prev = b_end + a_end
        a = jnp.maximum(mp_s[d], um)
        a_s[d] = a
        sig_s[d] = b + a

    ones = jnp.ones((tc, LANES), BF16)
    row_i = lax.broadcasted_iota(jnp.int32, (tc, tc), 0)
    col_i = lax.broadcasted_iota(jnp.int32, (tc, tc), 1)

    def to_col(row):
        return jnp.transpose(jnp.broadcast_to(row, (LANES, tc)))

    def chunk(d, c):
        reverse = d == 1
        tok = pl.ds(pl.multiple_of(c * tc, tc), tc)
        q = q_ref[tok, :]
        kt = kt_ref[pl.ds(pl.multiple_of(c * M_QK, M_QK), M_QK), :]
        va = jnp.concatenate([v_ref[tok, :], ones], axis=1)
        u = u_s[d, pl.ds(c, 1), :]
        a_col = to_col(a_s[d, pl.ds(c, 1), :])
        sig_col = to_col(sig_s[d, pl.ds(c, 1), :])
        m_prev = mp_s[d, pl.ds(c, 1), :]
        a_end = al_s[d, pl.ds(c, 1), :]

        s = jnp.dot(q, kt.astype(BF16), preferred_element_type=F32)
        causal = (col_i >= row_i) if reverse else (col_i <= row_i)
        arg = u - jnp.concatenate([a_col] * (tc // LANES), axis=1)
        w = jnp.exp(jnp.where(causal, arg, NEG))
        intra = jnp.dot((s * w).astype(BF16), va, preferred_element_type=F32)
        c_aug = c_ref[d]
        inter = jnp.dot(q, c_aug.astype(BF16), preferred_element_type=F32)
        w_inter = jnp.exp(m_prev[:, :LANES] - a_col)
        num = jnp.concatenate([w_inter] * (M_V // LANES), axis=1) * inter[:, :M_V] + intra[:, :M_V]
        den = w_inter * inter[:, M_V:] + intra[:, M_V:]
        inv = 1.0 / jnp.maximum(jnp.abs(den), jnp.exp(-sig_col))
        h = num * jnp.concatenate([inv] * (M_V // LANES), axis=1)
        wk = jnp.exp(u - a_end)
        upd = jnp.dot((kt * wk).astype(BF16), va, preferred_element_type=F32)
        decay = jnp.exp(m_prev[:, :LANES] - a_end[:, :LANES])
        c_ref[d] = jnp.concatenate([decay] * (c_aug.shape[1] // LANES), axis=1) * c_aug + upd
        return h

    c_ref[...] = jnp.zeros(c_ref.shape, F32)

    def fwd_body(c, carry):
        hbuf[pl.ds(pl.multiple_of(c * tc, tc), tc), :] = chunk(0, c)
        return carry

    lax.fori_loop(0, nchunk, fwd_body, 0)

    def bwd_body(i, carry):
        c = nchunk - 1 - i
        tok = pl.ds(pl.multiple_of(c * tc, tc), tc)
        hs = hbuf[tok, :] + chunk(1, c)
        ms = jnp.mean(hs * hs, axis=-1, keepdims=True)
        y = hs * lax.rsqrt(ms + NORM_EPS) * nw_ref[...]
        y = og_ref[tok, :].astype(F32) * y * zg_ref[tok, :].astype(F32)
        y_ref[tok, :] = y.astype(BF16)
        return carry

    lax.fori_loop(0, nchunk, bwd_body, 0)


def _mlstm(p, kt, gt, m_norm_w, batch, seq):
    nchunk = seq // MLSTM_CHUNK
    qm_col0 = N_MAIN_STEPS * PROJ_TN // M_QK
    vblk = lambda seg: pl.BlockSpec((seq, M_V), lambda b, h: (b, seg * (PROJ_TN // M_V) + h))
    rows = pltpu.VMEM((2, nchunk, MLSTM_CHUNK), F32)
    return pl.pallas_call(
        _mlstm_kernel,
        grid=(batch, M_HEADS),
        in_specs=[
            pl.BlockSpec((seq, M_QK), lambda b, h: (b, qm_col0 + h)),
            pl.BlockSpec((None, nchunk * M_QK, MLSTM_CHUNK), lambda b, h: (h, b, 0)),
            vblk(4), vblk(5), vblk(6),
            pl.BlockSpec((seq // LANES * N_GATES, LANES), lambda b, h: (b, 0)),
            pl.BlockSpec((1, M_V), lambda b, h: (0, h)),
        ],
        out_specs=pl.BlockSpec((seq, M_V), lambda b, h: (b, h)),
        out_shape=jax.ShapeDtypeStruct((batch * seq, M_HEADS * M_V), BF16),
        scratch_shapes=[
            pltpu.VMEM((seq, M_V), F32),
            pltpu.VMEM((2, M_QK, M_V + LANES), F32),
            rows, rows, rows, rows, rows,
        ],
        compiler_params=pltpu.CompilerParams(
            dimension_semantics=("arbitrary", "arbitrary"), vmem_limit_bytes=VMEM_LIMIT),
        name="mlstm",
    )(p, kt, p, p, p, gt, m_norm_w)


def _out_kernel(x_ref, ya_ref, ym_ref, wa_ref, wm_ref, o_ref):
    acc = jnp.dot(ya_ref[...], wa_ref[...], preferred_element_type=F32)
    acc = acc + jnp.dot(ym_ref[...], wm_ref[...], preferred_element_type=F32)
    o_ref[...] = x_ref[...] + acc


def _output_projection(x2, ya, ym, wa, wm):
    n_tok = x2.shape[0]
    tm = OUT_TM
    const = lambda m: (0, 0)
    one = pl.Buffered(1)
    return pl.pallas_call(
        _out_kernel,
        grid=(n_tok // tm,),
        in_specs=[
            pl.BlockSpec((tm, D_MODEL), lambda m: (m, 0)),
            pl.BlockSpec((tm, ya.shape[1]), lambda m: (m, 0)),
            pl.BlockSpec((tm, ym.shape[1]), lambda m: (m, 0)),
            pl.BlockSpec(wa.shape, const, pipeline_mode=one),
            pl.BlockSpec(wm.shape, const, pipeline_mode=one),
        ],
        out_specs=pl.BlockSpec((tm, D_MODEL), lambda m: (m, 0)),
        out_shape=jax.ShapeDtypeStruct((n_tok, D_MODEL), F32),
        compiler_params=pltpu.CompilerParams(
            dimension_semantics=("arbitrary",), vmem_limit_bytes=VMEM_LIMIT),
        name="out_proj",
    )(x2, ya, ym, wa, wm)


def _rope_tables(seq):
    pos = jnp.arange(seq, dtype=F32)
    inv_freq = ROPE_THETA ** (-jnp.arange(0, ROT_DIM, 2, dtype=F32) / ROT_DIM)
    ang = pos[:, None] * inv_freq[None, :]
    cos, sin = jnp.cos(ang), jnp.sin(ang)
    half = ROT_DIM // 2
    rest = HEAD_DIM - ROT_DIM
    cos_t = jnp.concatenate([cos, cos, jnp.ones((seq, rest), F32)], axis=1)
    s1_t = jnp.concatenate([-sin, jnp.zeros((seq, half + rest), F32)], axis=1)
    s2_t = jnp.concatenate([jnp.zeros((seq, half), F32), sin, jnp.zeros((seq, rest), F32)], axis=1)
    return cos_t, s1_t, s2_t


def _layer(x, norm_w, w_in, b_gates, q_norm_w, k_norm_w, m_norm_w, w_out, tables):
    batch, seq, _ = x.shape
    x2 = x.reshape(batch * seq, D_MODEL)
    a_end = 4 * D_ATT
    qm_end = a_end + M_HEADS * M_QK
    km_end = qm_end + M_HEADS * M_QK
    g0 = km_end + 3 * M_HEADS * M_V
    w_main = jnp.concatenate([w_in[:, :a_end], w_in[:, km_end:g0]], axis=1).astype(BF16)
    w_qm = w_in[:, a_end:qm_end].astype(BF16)
    w_t = jnp.concatenate([w_in[:, qm_end:km_end], w_in[:, g0:]], axis=1).T.astype(BF16)
    bias_bc = jnp.broadcast_to(b_gates.astype(F32).reshape(N_GATES, 1), (N_GATES, LANES))
    p, kt, gt = _input_projection(
        x2, norm_w.reshape(1, D_MODEL), w_main, w_qm, w_t, bias_bc,
        q_norm_w.reshape(1, HEAD_DIM), k_norm_w.reshape(1, HEAD_DIM), *tables, seq)
    ya = _attention(p, batch, seq)
    ym = _mlstm(p, kt, gt, m_norm_w.reshape(1, M_HEADS * M_V), batch, seq)
    wo = w_out.astype(BF16)
    out = _output_projection(x2, ya, ym, wo[:D_ATT], wo[D_ATT:])
    return out.reshape(x.shape)


def kernel(x, norm_w, w_in, b_gates, q_norm_w, k_norm_w, m_norm_w, w_out):
    tables = _rope_tables(x.shape[1])
    for layer in range(norm_w.shape[0]):
        x = _layer(x, norm_w[layer], w_in[layer], b_gates[layer], q_norm_w[layer],
                   k_norm_w[layer], m_norm_w[layer], w_out[layer], tables)
    return x
```

```python
import functools
import math

import jax
import jax.numpy as jnp
import numpy as np
from jax import lax
from jax.experimental import pallas as pl
from jax.experimental.pallas import tpu as pltpu

F32 = jnp.float32
BF16 = jnp.bfloat16

D_MODEL = 2048
D_ATT = 1024
ATT_HEADS = 8
HEAD_DIM = 128
ROT_DIM = 32
ROPE_THETA = 500000.0
N_SIDE = 64
M_HEADS = 4
M_QK = 128
M_V = 256
N_GATES = 16
NORM_EPS = 1e-6
NEG = -1e30

LANES = 128
MXU_COLS = 256
MLSTM_CHUNK = 256
PROJ_TM = 1024
PROJ_TN = 1024
N_MAIN_STEPS = 7
OUT_TM = 512
VMEM_LIMIT = 56 * 1024 * 1024


def _sigmoid(x):
    return 1.0 / (1.0 + jnp.exp(-x))


def _proj_kernel(x_ref, nw_ref, w_ref, wqm_ref, wt_ref, bias_ref, qw_ref, kw_ref,
                 cos_ref, sin_ref, avg_ref, rot_ref,
                 p_ref, kt_ref, gt_ref, xn_ref):
    n = pl.program_id(1)
    tm = x_ref.shape[0]

    @pl.when(n == 0)
    def _():
        x = x_ref[...]
        ms = jnp.mean(x * x, axis=-1, keepdims=True)
        xn_ref[...] = (x * lax.rsqrt(ms + NORM_EPS) * nw_ref[...]).astype(BF16)

    def column_chunks():
        width = 2 * MXU_COLS
        for j in range(PROJ_TN // width):
            acc = jnp.dot(xn_ref[...], w_ref[:, j * width:(j + 1) * width],
                          preferred_element_type=F32)
            for i in range(width // MXU_COLS):
                lo = j * width + i * MXU_COLS
                yield slice(lo, lo + MXU_COLS), acc[:, i * MXU_COLS:(i + 1) * MXU_COLS]

    @pl.when(n < 2)
    def _():
        w_row = jnp.where(n == 0, qw_ref[...] * (HEAD_DIM ** -0.5), kw_ref[...])
        for sl, acc in column_chunks():
            ms = jnp.dot((acc * acc).astype(BF16), avg_ref[...], preferred_element_type=F32)
            b = acc * w_row
            partner = jnp.dot(b.astype(BF16), rot_ref[...], preferred_element_type=F32)
            r = (b * cos_ref[...] + partner * sin_ref[...]) * lax.rsqrt(ms + NORM_EPS)
            p_ref[:, sl] = r.astype(BF16)

    @pl.when((n >= 2) & (n < N_MAIN_STEPS))
    def _():
        plain = (n == 2) | (n == 4)
        sig_only = n == 5
        for sl, acc in column_chunks():
            sg = _sigmoid(acc)
            p_ref[:, sl] = jnp.where(plain, acc, sg * jnp.where(sig_only, 1.0, acc)).astype(BF16)

    @pl.when(n == N_MAIN_STEPS)
    def _():
        xn = xn_ref[...]
        qm = jnp.dot(xn, wqm_ref[...], preferred_element_type=F32)
        half = M_HEADS * M_QK
        p_ref[:, :half] = qm.astype(BF16)
        p_ref[:, half:] = jnp.zeros((tm, PROJ_TN - half), BF16)
        t = lax.dot_general(wt_ref[...], xn, (((1,), (1,)), ((), ())),
                            preferred_element_type=F32)
        for hh in range(M_HEADS):
            for j in range(tm // MLSTM_CHUNK):
                kt_ref[hh, j * M_QK:(j + 1) * M_QK, :] = (
                    t[hh * M_QK:(hh + 1) * M_QK, j * MLSTM_CHUNK:(j + 1) * MLSTM_CHUNK]
                    * (M_QK ** -0.5))
        g = t[half:half + N_GATES, :]
        for j in range(tm // LANES):
            gt_ref[j * N_GATES:(j + 1) * N_GATES, :] = (
                g[:, j * LANES:(j + 1) * LANES] + bias_ref[...])


def _main_weight_block(n):
    seg = jnp.minimum(n, N_MAIN_STEPS - 1)
    return seg + (seg >= 4).astype(jnp.int32)


def _input_projection(x2, norm_w, w_bf, w_t, bias_bc, qw, kw, cos_t, sin_t, avg_m, rot_m, seq):
    n_tok = x2.shape[0]
    tm, tn = PROJ_TM, PROJ_TN
    tiles_per_seq = seq // tm
    const = lambda m, n: (0, 0)
    grid = (n_tok // tm, N_MAIN_STEPS + 1)
    one = pl.Buffered(1)
    qm_cols = M_HEADS * M_QK
    qm_block = 4 * D_ATT // qm_cols
    return pl.pallas_call(
        _proj_kernel,
        grid=grid,
        in_specs=[
            pl.BlockSpec((tm, D_MODEL), lambda m, n: (m, 0)),
            pl.BlockSpec((1, D_MODEL), const, pipeline_mode=one),
            pl.BlockSpec((D_MODEL, tn), lambda m, n: (0, _main_weight_block(n))),
            pl.BlockSpec((D_MODEL, qm_cols), lambda m, n: (0, qm_block), pipeline_mode=one),
            pl.BlockSpec(w_t.shape, const, pipeline_mode=one),
            pl.BlockSpec(bias_bc.shape, const, pipeline_mode=one),
            pl.BlockSpec((1, MXU_COLS), const, pipeline_mode=one),
            pl.BlockSpec((1, MXU_COLS), const, pipeline_mode=one),
            pl.BlockSpec((tm, MXU_COLS), lambda m, n: (m % tiles_per_seq, 0)),
            pl.BlockSpec((tm, MXU_COLS), lambda m, n: (m % tiles_per_seq, 0)),
            pl.BlockSpec((MXU_COLS, MXU_COLS), const, pipeline_mode=one),
            pl.BlockSpec((MXU_COLS, MXU_COLS), const, pipeline_mode=one),
        ],
        out_specs=[
            pl.BlockSpec((tm, tn), lambda m, n: (m, n)),
            pl.BlockSpec((M_HEADS, tm // MLSTM_CHUNK * M_QK, MLSTM_CHUNK), lambda m, n: (0, m, 0)),
            pl.BlockSpec((tm // LANES * N_GATES, LANES), lambda m, n: (m, 0)),
        ],
        out_shape=[
            jax.ShapeDtypeStruct((n_tok, (N_MAIN_STEPS + 1) * tn), BF16),
            jax.ShapeDtypeStruct((M_HEADS, n_tok // MLSTM_CHUNK * M_QK, MLSTM_CHUNK), F32),
            jax.ShapeDtypeStruct((n_tok // LANES * N_GATES, LANES), F32),
        ],
        scratch_shapes=[pltpu.VMEM((tm, D_MODEL), BF16)],
        compiler_params=pltpu.CompilerParams(
            dimension_semantics=("arbitrary", "arbitrary"), vmem_limit_bytes=VMEM_LIMIT),
        name="in_proj",
    )(x2, norm_w, w_bf, w_bf, w_t, bias_bc, qw, kw, cos_t, sin_t, avg_m, rot_m)


ATT_QT = 128
ATT_KT = ATT_QT + 2 * N_SIDE
ATT_GROUP = 8


def _band_bias(nq, nk, off):
    rel = (lax.broadcasted_iota(jnp.int32, (nq, nk), 1)
           - lax.broadcasted_iota(jnp.int32, (nq, nk), 0)) + off
    return jnp.where(jnp.abs(rel) <= N_SIDE, 0.0, NEG).astype(F32)


def _attn_kernel(q_ref, k_ref, v_ref, z_ref, o_ref, qf, kf, vf, acc_s, m_s, l_s, bias_s, bias16_s):
    seq = q_ref.shape[0]
    qf[...] = q_ref[...].astype(F32)
    kf[...] = k_ref[...].astype(F32)
    vf[...] = v_ref[...].astype(F32)
    for j in range(3):
        bias_s[j] = _band_bias(ATT_QT, ATT_KT, -N_SIDE * j)
    nq16 = seq // 16
    bias16_s[...] = _band_bias(nq16, nq16, 0)
    ones = jnp.ones((ATT_KT, HEAD_DIM), BF16)

    def tile(q, k, v, bias, old):
        s = lax.dot_general(q, k, (((1,), (1,)), ((), ())), preferred_element_type=F32) + bias
        m_t = jnp.max(s, axis=1, keepdims=True)
        if old is None:
            m_n = jnp.broadcast_to(m_t, (q.shape[0], HEAD_DIM))
        else:
            m_o, l_o, a_o = old
            m_n = jnp.maximum(m_o, m_t)
        p = jnp.exp(s - jnp.concatenate([m_n] * (s.shape[1] // HEAD_DIM), axis=1)).astype(BF16)
        r = jnp.dot(p, jnp.concatenate([v, ones], axis=1), preferred_element_type=F32)
        a_t, l_t = r[:, :HEAD_DIM], r[:, HEAD_DIM:]
        if old is None:
            return m_n, l_t, a_t
        alpha = jnp.exp(m_o - m_n)
        return m_n, alpha * l_o + l_t, alpha * a_o + a_t

    def strided(ref, start, size, dil):
        return ref[pl.ds(start, size, stride=dil), :].astype(BF16)

    def tile_geometry(i, dil):
        length = seq // dil
        kl = jnp.clip(ATT_QT * i - N_SIDE, 0, length - ATT_KT)
        return kl, (ATT_QT * i - kl) // N_SIDE

    dil = 16

    def body16(rg, carry):
        for rr in range(ATT_GROUP // 2):
            r = rg * (ATT_GROUP // 2) + rr
            rs = pl.ds(r, nq16, stride=dil)
            m, l, a = tile(strided(qf, r, nq16, dil), strided(kf, r, nq16, dil),
                           strided(vf, r, nq16, dil), bias16_s[...], None)
            m_s[rs, :] = m
            l_s[rs, :] = l
            acc_s[rs, :] = a
        return carry

    lax.fori_loop(0, dil // (ATT_GROUP // 2), body16, 0)

    dil = 4

    def body4(ig, carry):
        for ii in range(ATT_GROUP // dil):
            i = ig * (ATT_GROUP // dil) + ii
            kl, bi = tile_geometry(i, dil)
            bias = bias_s[bi]
            for r in range(dil):
                q0 = r + dil * ATT_QT * i
                k0 = r + dil * kl
                qs = pl.ds(q0, ATT_QT, stride=dil)
                m, l, a = tile(strided(qf, q0, ATT_QT, dil), strided(kf, k0, ATT_KT, dil),
                               strided(vf, k0, ATT_KT, dil), bias,
                               (m_s[qs, :], l_s[qs, :], acc_s[qs, :]))
                m_s[qs, :] = m
                l_s[qs, :] = l
                acc_s[qs, :] = a
        return carry

    lax.fori_loop(0, seq // dil // ATT_QT // (ATT_GROUP // dil), body4, 0)

    def body1(ig, carry):
        for ii in range(ATT_GROUP):
            i = ig * ATT_GROUP + ii
            kl, bi = tile_geometry(i, 1)
            qs = pl.ds(pl.multiple_of(ATT_QT * i, ATT_QT), ATT_QT)
            ks = pl.ds(pl.multiple_of(kl, N_SIDE), ATT_KT)
            _, l, a = tile(q_ref[qs, :], k_ref[ks, :], v_ref[ks, :], bias_s[bi],
                           (m_s[qs, :], l_s[qs, :], acc_s[qs, :]))
            o_ref[qs, :] = (a / l * z_ref[qs, :].astype(F32)).astype(BF16)
        return carry

    lax.fori_loop(0, seq // ATT_QT // ATT_GROUP, body1, 0)


def _attention(p, batch, seq):
    blk = lambda col0: pl.BlockSpec((seq, HEAD_DIM), lambda b, h: (b, col0 + h))
    return pl.pallas_call(
        _attn_kernel,
        grid=(batch, ATT_HEADS),
        in_specs=[blk(0), blk(ATT_HEADS), blk(2 * ATT_HEADS), blk(3 * ATT_HEADS)],
        out_specs=pl.BlockSpec((seq, HEAD_DIM), lambda b, h: (b, h)),
        out_shape=jax.ShapeDtypeStruct((batch * seq, D_ATT), BF16),
        scratch_shapes=[pltpu.VMEM((seq, HEAD_DIM), F32) for _ in range(6)] + [
            pltpu.VMEM((3, ATT_QT, ATT_KT), F32), pltpu.VMEM((seq // 16, seq // 16), F32)],
        compiler_params=pltpu.CompilerParams(
            dimension_semantics=("arbitrary", "arbitrary"), vmem_limit_bytes=VMEM_LIMIT),
        name="dilated_attn",
    )(p, p, p, p)


def _scan_lanes(x, op, identity, reverse):
    width = x.shape[1]
    pos = lax.broadcasted_iota(jnp.int32, x.shape, 1)
    k = 1
    while k < width:
        if reverse:
            shifted = jnp.where(pos < width - k, pltpu.roll(x, width - k, 1), identity)
        else:
            shifted = jnp.where(pos >= k, pltpu.roll(x, k, 1), identity)
        x = op(x, shifted)
        k *= 2
    return x


def _mlstm_kernel(q_ref, kt_ref, v_ref, og_ref, zg_ref, g_ref, nw_ref, y_ref,
                  hbuf, c_ref, u_s, a_s, sig_s, mp_s, al_s):
    head = pl.program_id(1)
    seq = q_ref.shape[0]
    tc = MLSTM_CHUNK
    nchunk = seq // tc
    blocks_per_chunk = tc // LANES

    def gate_rows(g):
        halves = [g_ref[pl.ds(hf * N_GATES + g, nchunk, stride=blocks_per_chunk * N_GATES), :]
                  for hf in range(blocks_per_chunk)]
        return jnp.concatenate(halves, axis=1)

    for d in range(2):
        reverse = d == 1
        li = gate_rows(2 * d * M_HEADS + head)
        fp = gate_rows((2 * d + 1) * M_HEADS + head)
        lf = jnp.minimum(fp, 0.0) - jnp.log1p(jnp.exp(-jnp.abs(fp)))
        b = _scan_lanes(lf, jnp.add, 0.0, reverse)
        u = li - b
        um = _scan_lanes(u, jnp.maximum, NEG, reverse)
        u_s[d] = u
        m_prev = jnp.full((1, tc), NEG, F32)
        order = range(nchunk - 1, -1, -1) if reverse else range(nchunk)
        for c in order:
            u_max = jnp.max(u[c:c + 1, :], axis=1, keepdims=True)
            b_end = jnp.sum(lf[c:c + 1, :], axis=1, keepdims=True)
            a_end = jnp.maximum(m_prev, u_max)
            mp_s[d, c:c + 1, :] = m_prev
            al_s[d, c:c + 1, :] = a_end
            m_prev = b_end + a_end
        a = jnp.maximum(mp_s[d], um)
        a_s[d] = a
        sig_s[d] = b + a

    ones = jnp.ones((tc, LANES), BF16)
    row_i = lax.broadcasted_iota(jnp.int32, (tc, tc), 0)
    col_i = lax.broadcasted_iota(jnp.int32, (tc, tc), 1)

    def to_col(row):
        return jnp.transpose(jnp.broadcast_to(row, (LANES, tc)))

    def chunk(d, c):
        reverse = d == 1
        tok = pl.ds(pl.multiple_of(c * tc, tc), tc)
        q = q_ref[tok, :]
        kt = kt_ref[pl.ds(pl.multiple_of(c * M_QK, M_QK), M_QK), :]
        va = jnp.concatenate([v_ref[tok, :], ones], axis=1)
        u = u_s[d, pl.ds(c, 1), :]
        a_col = to_col(a_s[d, pl.ds(c, 1), :])
        sig_col = to_col(sig_s[d, pl.ds(c, 1), :])
        m_prev = mp_s[d, pl.ds(c, 1), :]
        a_end = al_s[d, pl.ds(c, 1), :]

        s = jnp.dot(q, kt.astype(BF16), preferred_element_type=F32)
        causal = (col_i >= row_i) if reverse else (col_i <= row_i)
        arg = u - jnp.concatenate([a_col] * (tc // LANES), axis=1)
        w = jnp.exp(jnp.where(causal, arg, NEG))
        intra = jnp.dot((s * w).astype(BF16), va, preferred_element_type=F32)
        c_aug = c_ref[d]
        inter = jnp.dot(q, c_aug.astype(BF16), preferred_element_type=F32)
        w_inter = jnp.exp(m_prev[:, :LANES] - a_col)
        num = jnp.concatenate([w_inter] * (M_V // LANES), axis=1) * inter[:, :M_V] + intra[:, :M_V]
        den = w_inter * inter[:, M_V:] + intra[:, M_V:]
        inv = 1.0 / jnp.maximum(jnp.abs(den), jnp.exp(-sig_col))
        h = num * jnp.concatenate([inv] * (M_V // LANES), axis=1)
        wk = jnp.exp(u - a_end)
        upd = jnp.dot((kt * wk).astype(BF16), va, preferred_element_type=F32)
        decay = jnp.exp(m_prev[:, :LANES] - a_end[:, :LANES])
        c_ref[d] = jnp.concatenate([decay] * (c_aug.shape[1] // LANES), axis=1) * c_aug + upd
        return h

    c_ref[...] = jnp.zeros(c_ref.shape, F32)

    def finalize(c, h):
        tok = pl.ds(pl.multiple_of(c * tc, tc), tc)
        hs = hbuf[tok, :] + h
        ms = jnp.mean(hs * hs, axis=-1, keepdims=True)
        y = hs * lax.rsqrt(ms + NORM_EPS) * nw_ref[...]
        y = og_ref[tok, :].astype(F32) * y * zg_ref[tok, :].astype(F32)
        y_ref[tok, :] = y.astype(BF16)

    def first_half(i, carry):
        cb = nchunk - 1 - i
        hbuf[pl.ds(pl.multiple_of(i * tc, tc), tc), :] = chunk(0, i)
        hbuf[pl.ds(pl.multiple_of(cb * tc, tc), tc), :] = chunk(1, cb)
        return carry

    def second_half(i, carry):
        cb = nchunk - 1 - i
        finalize(i, chunk(0, i))
        finalize(cb, chunk(1, cb))
        return carry

    lax.fori_loop(0, nchunk // 2, first_half, 0)
    lax.fori_loop(nchunk // 2, nchunk, second_half, 0)


def _mlstm(p, kt, gt, m_norm_w, batch, seq):
    nchunk = seq // MLSTM_CHUNK
    qm_col0 = N_MAIN_STEPS * PROJ_TN // M_QK
    vblk = lambda seg: pl.BlockSpec((seq, M_V), lambda b, h: (b, seg * (PROJ_TN // M_V) + h))
    rows = pltpu.VMEM((2, nchunk, MLSTM_CHUNK), F32)
    return pl.pallas_call(
        _mlstm_kernel,
        grid=(batch, M_HEADS),
        in_specs=[
            pl.BlockSpec((seq, M_QK), lambda b, h: (b, qm_col0 + h)),
            pl.BlockSpec((None, nchunk * M_QK, MLSTM_CHUNK), lambda b, h: (h, b, 0)),
            vblk(4), vblk(5), vblk(6),
            pl.BlockSpec((seq // LANES * N_GATES, LANES), lambda b, h: (b, 0)),
            pl.BlockSpec((1, M_V), lambda b, h: (0, h)),
        ],
        out_specs=pl.BlockSpec((seq, M_V), lambda b, h: (b, h)),
        out_shape=jax.ShapeDtypeStruct((batch * seq, M_HEADS * M_V), BF16),
        scratch_shapes=[
            pltpu.VMEM((seq, M_V), F32),
            pltpu.VMEM((2, M_QK, M_V + LANES), F32),
            rows, rows, rows, rows, rows,
        ],
        compiler_params=pltpu.CompilerParams(
            dimension_semantics=("arbitrary", "arbitrary"), vmem_limit_bytes=VMEM_LIMIT),
        name="mlstm",
    )(p, kt, p, p, p, gt, m_norm_w)


def _out_kernel(x_ref, ya_ref, ym_ref, wa_ref, wm_ref, o_ref):
    acc = jnp.dot(ya_ref[...], wa_ref[...], preferred_element_type=F32)
    acc = acc + jnp.dot(ym_ref[...], wm_ref[...], preferred_element_type=F32)
    o_ref[...] = x_ref[...] + acc


def _output_projection(x2, ya, ym, wo):
    n_tok = x2.shape[0]
    tm = OUT_TM
    one = pl.Buffered(1)
    return pl.pallas_call(
        _out_kernel,
        grid=(n_tok // tm,),
        in_specs=[
            pl.BlockSpec((tm, D_MODEL), lambda m: (m, 0)),
            pl.BlockSpec((tm, ya.shape[1]), lambda m: (m, 0)),
            pl.BlockSpec((tm, ym.shape[1]), lambda m: (m, 0)),
            pl.BlockSpec((ya.shape[1], D_MODEL), lambda m: (0, 0), pipeline_mode=one),
            pl.BlockSpec((ym.shape[1], D_MODEL), lambda m: (1, 0), pipeline_mode=one),
        ],
        out_specs=pl.BlockSpec((tm, D_MODEL), lambda m: (m, 0)),
        out_shape=jax.ShapeDtypeStruct((n_tok, D_MODEL), F32),
        compiler_params=pltpu.CompilerParams(
            dimension_semantics=("arbitrary",), vmem_limit_bytes=VMEM_LIMIT),
        name="out_proj",
    )(x2, ya, ym, wo, wo)


def _rope_tables(seq):
    pos = jnp.arange(seq, dtype=F32)
    inv_freq = ROPE_THETA ** (-jnp.arange(0, ROT_DIM, 2, dtype=F32) / ROT_DIM)
    ang = pos[:, None] * inv_freq[None, :]
    cos, sin = jnp.cos(ang), jnp.sin(ang)
    rest = HEAD_DIM - ROT_DIM
    heads = MXU_COLS // HEAD_DIM
    cos_t = jnp.concatenate([cos, cos, jnp.ones((seq, rest), F32)] * heads, axis=1)
    sin_t = jnp.concatenate([sin, sin, jnp.zeros((seq, rest), F32)] * heads, axis=1)
    return cos_t, sin_t


def _head_matrices():
    half = ROT_DIM // 2
    src = np.arange(MXU_COLS)[:, None]
    dst = np.arange(MXU_COLS)[None, :]
    same_head = (src // HEAD_DIM) == (dst // HEAD_DIM)
    avg = np.where(same_head, 1.0 / HEAD_DIM, 0.0)
    d = dst % HEAD_DIM
    rot = (np.where(same_head & (d < half) & (src == dst + half), -1.0, 0.0)
           + np.where(same_head & (d >= half) & (d < ROT_DIM) & (src == dst - half), 1.0, 0.0))
    return jnp.asarray(avg, BF16), jnp.asarray(rot, BF16)


def _layer(x, norm_w, w_in, b_gates, q_norm_w, k_norm_w, m_norm_w, w_out, tables):
    batch, seq, _ = x.shape
    x2 = x.reshape(batch * seq, D_MODEL)
    a_end = 4 * D_ATT
    qm_end = a_end + M_HEADS * M_QK
    km_end = qm_end + M_HEADS * M_QK
    g0 = km_end + 3 * M_HEADS * M_V
    w_bf = w_in.astype(BF16)
    w_t = jnp.concatenate([w_in[:, qm_end:km_end], w_in[:, g0:]], axis=1).T.astype(BF16)
    bias_bc = jnp.broadcast_to(b_gates.astype(F32).reshape(N_GATES, 1), (N_GATES, LANES))
    heads = MXU_COLS // HEAD_DIM
    p, kt, gt = _input_projection(
        x2, norm_w.reshape(1, D_MODEL), w_bf, w_t, bias_bc,
        jnp.tile(q_norm_w.reshape(1, HEAD_DIM), (1, heads)),
        jnp.tile(k_norm_w.reshape(1, HEAD_DIM), (1, heads)), *tables, seq)
    ya = _attention(p, batch, seq)
    ym = _mlstm(p, kt, gt, m_norm_w.reshape(1, M_HEADS * M_V), batch, seq)
    out = _output_projection(x2, ya, ym, w_out.astype(BF16))
    return out.reshape(x.shape)


def kernel(x, norm_w, w_in, b_gates, q_norm_w, k_norm_w, m_norm_w, w_out):
    tables = _rope_tables(x.shape[1]) + _head_matrices()
    for layer in range(norm_w.shape[0]):
        x = _layer(x, norm_w[layer], w_in[layer], b_gates[layer], q_norm_w[layer],
                   k_norm_w[layer], m_norm_w[layer], w_out[layer], tables)
    return x
```

```python
import functools
import math

import jax
import jax.numpy as jnp
import numpy as np
from jax import lax
from jax.experimental import pallas as pl
from jax.experimental.pallas import tpu as pltpu

F32 = jnp.float32
BF16 = jnp.bfloat16

D_MODEL = 2048
D_ATT = 1024
ATT_HEADS = 8
HEAD_DIM = 128
ROT_DIM = 32
ROPE_THETA = 500000.0
N_SIDE = 64
M_HEADS = 4
M_QK = 128
M_V = 256
N_GATES = 16
NORM_EPS = 1e-6
NEG = -1e30

LANES = 128
MXU_COLS = 256
MLSTM_CHUNK = 256
PROJ_TM = 1024
PROJ_TN = 1024
N_MAIN_STEPS = 7
OUT_TM = 512
VMEM_LIMIT = 56 * 1024 * 1024


def _sigmoid(x):
    return 1.0 / (1.0 + jnp.exp(-x))


def _proj_kernel(x_ref, nw_ref, w_ref, wqk_ref, wg_ref, bias_ref, qw_ref, kw_ref,
                 cos_ref, sin_ref, avg_ref, rot_ref,
                 p_ref, kt_ref, gt_ref, xn_ref):
    n = pl.program_id(1)
    tm = x_ref.shape[0]

    @pl.when(n == 0)
    def _():
        x = x_ref[...]
        ms = jnp.mean(x * x, axis=-1, keepdims=True)
        xn_ref[...] = (x * lax.rsqrt(ms + NORM_EPS) * nw_ref[...]).astype(BF16)

    def column_chunks():
        width = 2 * MXU_COLS
        for j in range(PROJ_TN // width):
            acc = jnp.dot(xn_ref[...], w_ref[:, j * width:(j + 1) * width],
                          preferred_element_type=F32)
            for i in range(width // MXU_COLS):
                lo = j * width + i * MXU_COLS
                yield slice(lo, lo + MXU_COLS), acc[:, i * MXU_COLS:(i + 1) * MXU_COLS]

    @pl.when(n < 2)
    def _():
        w_row = jnp.where(n == 0, qw_ref[...] * (HEAD_DIM ** -0.5), kw_ref[...])
        for sl, acc in column_chunks():
            ms = jnp.dot((acc * acc).astype(BF16), avg_ref[...], preferred_element_type=F32)
            b = acc * w_row
            partner = jnp.dot(b.astype(BF16), rot_ref[...], preferred_element_type=F32)
            r = (b * cos_ref[...] + partner * sin_ref[...]) * lax.rsqrt(ms + NORM_EPS)
            p_ref[:, sl] = r.astype(BF16)

    @pl.when((n >= 2) & (n < N_MAIN_STEPS))
    def _():
        plain = (n == 2) | (n == 4)
        sig_only = n == 5
        for sl, acc in column_chunks():
            sg = _sigmoid(acc)
            p_ref[:, sl] = jnp.where(plain, acc, sg * jnp.where(sig_only, 1.0, acc)).astype(BF16)

    @pl.when(n == N_MAIN_STEPS)
    def _():
        xn = xn_ref[...]
        half = M_HEADS * M_QK
        qk = jnp.dot(xn, wqk_ref[...], preferred_element_type=F32)
        p_ref[:, :half] = qk[:, :half].astype(BF16)
        p_ref[:, half:] = jnp.zeros((tm, PROJ_TN - half), BF16)
        for hh in range(M_HEADS):
            kt = jnp.transpose(qk[:, half + hh * M_QK:half + (hh + 1) * M_QK]) * (M_QK ** -0.5)
            for j in range(tm // MLSTM_CHUNK):
                kt_ref[hh, j * M_QK:(j + 1) * M_QK, :] = kt[:, j * MLSTM_CHUNK:(j + 1) * MLSTM_CHUNK]
        g = jnp.transpose(jnp.dot(xn, wg_ref[...], preferred_element_type=F32))
        for j in range(tm // LANES):
            gt_ref[j * N_GATES:(j + 1) * N_GATES, :] = (
                g[:N_GATES, j * LANES:(j + 1) * LANES] + bias_ref[...])


def _main_weight_block(n):
    seg = jnp.minimum(n, N_MAIN_STEPS - 1)
    return seg + (seg >= 4).astype(jnp.int32)


def _input_projection(x2, norm_w, w_bf, w_g, bias_bc, qw, kw, cos_t, sin_t, avg_m, rot_m, seq):
    n_tok = x2.shape[0]
    tm, tn = PROJ_TM, PROJ_TN
    tiles_per_seq = seq // tm
    const = lambda m, n: (0, 0)
    grid = (n_tok // tm, N_MAIN_STEPS + 1)
    one = pl.Buffered(1)
    qk_block = 4 * D_ATT // tn
    return pl.pallas_call(
        _proj_kernel,
        grid=grid,
        in_specs=[
            pl.BlockSpec((tm, D_MODEL), lambda m, n: (m, 0)),
            pl.BlockSpec((1, D_MODEL), const, pipeline_mode=one),
            pl.BlockSpec((D_MODEL, tn), lambda m, n: (0, _main_weight_block(n))),
            pl.BlockSpec((D_MODEL, tn), lambda m, n: (0, qk_block), pipeline_mode=one),
            pl.BlockSpec(w_g.shape, const, pipeline_mode=one),
            pl.BlockSpec(bias_bc.shape, const, pipeline_mode=one),
            pl.BlockSpec((1, MXU_COLS), const, pipeline_mode=one),
            pl.BlockSpec((1, MXU_COLS), const, pipeline_mode=one),
            pl.BlockSpec((tm, MXU_COLS), lambda m, n: (m % tiles_per_seq, 0)),
            pl.BlockSpec((tm, MXU_COLS), lambda m, n: (m % tiles_per_seq, 0)),
            pl.BlockSpec((MXU_COLS, MXU_COLS), const, pipeline_mode=one),
            pl.BlockSpec((MXU_COLS, MXU_COLS), const, pipeline_mode=one),
        ],
        out_specs=[
            pl.BlockSpec((tm, tn), lambda m, n: (m, n)),
            pl.BlockSpec((M_HEADS, tm // MLSTM_CHUNK * M_QK, MLSTM_CHUNK), lambda m, n: (0, m, 0)),
            pl.BlockSpec((tm // LANES * N_GATES, LANES), lambda m, n: (m, 0)),
        ],
        out_shape=[
            jax.ShapeDtypeStruct((n_tok, (N_MAIN_STEPS + 1) * tn), BF16),
            jax.ShapeDtypeStruct((M_HEADS, n_tok // MLSTM_CHUNK * M_QK, MLSTM_CHUNK), F32),
            jax.ShapeDtypeStruct((n_tok // LANES * N_GATES, LANES), F32),
        ],
        scratch_shapes=[pltpu.VMEM((tm, D_MODEL), BF16)],
        compiler_params=pltpu.CompilerParams(
            dimension_semantics=("arbitrary", "arbitrary"), vmem_limit_bytes=VMEM_LIMIT),
        name="in_proj",
    )(x2, norm_w, w_bf, w_bf, w_g, bias_bc, qw, kw, cos_t, sin_t, avg_m, rot_m)


ATT_QT = 128
ATT_KT = ATT_QT + 2 * N_SIDE
ATT_GROUP = 8


def _band_bias(nq, nk, off):
    rel = (lax.broadcasted_iota(jnp.int32, (nq, nk), 1)
           - lax.broadcasted_iota(jnp.int32, (nq, nk), 0)) + off
    return jnp.where(jnp.abs(rel) <= N_SIDE, 0.0, NEG).astype(F32)


def _attn_kernel(q_ref, k_ref, v_ref, z_ref, o_ref, qf, kf, vf, acc_s, m_s, l_s, bias_s, bias16_s):
    seq = q_ref.shape[0]
    qf[...] = q_ref[...].astype(F32)
    kf[...] = k_ref[...].astype(F32)
    vf[...] = v_ref[...].astype(F32)
    for j in range(3):
        bias_s[j] = _band_bias(ATT_QT, ATT_KT, -N_SIDE * j)
    nq16 = seq // 16
    bias16_s[...] = _band_bias(nq16, nq16, 0)
    ones = jnp.ones((ATT_KT, HEAD_DIM), BF16)

    def tile(q, k, v, bias, old):
        s = lax.dot_general(q, k, (((1,), (1,)), ((), ())), preferred_element_type=F32) + bias
        m_t = jnp.max(s, axis=1, keepdims=True)
        if old is None:
            m_n = jnp.broadcast_to(m_t, (q.shape[0], HEAD_DIM))
        else:
            m_o, l_o, a_o = old
            m_n = jnp.maximum(m_o, m_t)
        p = jnp.exp(s - jnp.concatenate([m_n] * (s.shape[1] // HEAD_DIM), axis=1)).astype(BF16)
        r = jnp.dot(p, jnp.concatenate([v, ones], axis=1), preferred_element_type=F32)
        a_t, l_t = r[:, :HEAD_DIM], r[:, HEAD_DIM:]
        if old is None:
            return m_n, l_t, a_t
        alpha = jnp.exp(m_o - m_n)
        return m_n, alpha * l_o + l_t, alpha * a_o + a_t

    def strided(ref, start, size, dil):
        return ref[pl.ds(start, size, stride=dil), :].astype(BF16)

    def tile_geometry(i, dil):
        length = seq // dil
        kl = jnp.clip(ATT_QT * i - N_SIDE, 0, length - ATT_KT)
        return kl, (ATT_QT * i - kl) // N_SIDE

    dil = 16

    def body16(rg, carry):
        for rr in range(ATT_GROUP // 2):
            r = rg * (ATT_GROUP // 2) + rr
            rs = pl.ds(r, nq16, stride=dil)
            m, l, a = tile(strided(qf, r, nq16, dil), strided(kf, r, nq16, dil),
                           strided(vf, r, nq16, dil), bias16_s[...], None)
            m_s[rs, :] = m
            l_s[rs, :] = l
            acc_s[rs, :] = a
        return carry

    lax.fori_loop(0, dil // (ATT_GROUP // 2), body16, 0)

    dil = 4

    def body4(ig, carry):
        for ii in range(ATT_GROUP // dil):
            i = ig * (ATT_GROUP // dil) + ii
            kl, bi = tile_geometry(i, dil)
            bias = bias_s[bi]
            for r in range(dil):
                q0 = r + dil * ATT_QT * i
                k0 = r + dil * kl
                qs = pl.ds(q0, ATT_QT, stride=dil)
                m, l, a = tile(strided(qf, q0, ATT_QT, dil), strided(kf, k0, ATT_KT, dil),
                               strided(vf, k0, ATT_KT, dil), bias,
                               (m_s[qs, :], l_s[qs, :], acc_s[qs, :]))
                m_s[qs, :] = m
                l_s[qs, :] = l
                acc_s[qs, :] = a
        return carry

    lax.fori_loop(0, seq // dil // ATT_QT // (ATT_GROUP // dil), body4, 0)

    def body1(ig, carry):
        for ii in range(ATT_GROUP):
            i = ig * ATT_GROUP + ii
            kl, bi = tile_geometry(i, 1)
            qs = pl.ds(pl.multiple_of(ATT_QT * i, ATT_QT), ATT_QT)
            ks = pl.ds(pl.multiple_of(kl, N_SIDE), ATT_KT)
            _, l, a = tile(q_ref[qs, :], k_ref[ks, :], v_ref[ks, :], bias_s[bi],
                           (m_s[qs, :], l_s[qs, :], acc_s[qs, :]))
            o_ref[qs, :] = (a / l * z_ref[qs, :].astype(F32)).astype(BF16)
        return carry

    lax.fori_loop(0, seq // ATT_QT // ATT_GROUP, body1, 0)


def _attention(p, batch, seq):
    blk = lambda col0: pl.BlockSpec((seq, HEAD_DIM), lambda b, h: (b, col0 + h))
    return pl.pallas_call(
        _attn_kernel,
        grid=(batch, ATT_HEADS),
        in_specs=[blk(0), blk(ATT_HEADS), blk(2 * ATT_HEADS), blk(3 * ATT_HEADS)],
        out_specs=pl.BlockSpec((seq, HEAD_DIM), lambda b, h: (b, h)),
        out_shape=jax.ShapeDtypeStruct((batch * seq, D_ATT), BF16),
        scratch_shapes=[pltpu.VMEM((seq, HEAD_DIM), F32) for _ in range(6)] + [
            pltpu.VMEM((3, ATT_QT, ATT_KT), F32), pltpu.VMEM((seq // 16, seq // 16), F32)],
        compiler_params=pltpu.CompilerParams(
            dimension_semantics=("arbitrary", "arbitrary"), vmem_limit_bytes=VMEM_LIMIT),
        name="dilated_attn",
    )(p, p, p, p)


def _scan_lanes(x, op, identity, reverse):
    width = x.shape[1]
    pos = lax.broadcasted_iota(jnp.int32, x.shape, 1)
    k = 1
    while k < width:
        if reverse:
            shifted = jnp.where(pos < width - k, pltpu.roll(x, width - k, 1), identity)
        else:
            shifted = jnp.where(pos >= k, pltpu.roll(x, k, 1), identity)
        x = op(x, shifted)
        k *= 2
    return x


def _mlstm_kernel(q_ref, kt_ref, v_ref, og_ref, zg_ref, g_ref, nw_ref, y_ref,
                  hbuf, c_ref, u_s, a_s, sig_s, mp_s, al_s):
    head = pl.program_id(1)
    seq = q_ref.shape[0]
    tc = MLSTM_CHUNK
    nchunk = seq // tc
    blocks_per_chunk = tc // LANES

    def gate_rows(g):
        halves = [g_ref[pl.ds(hf * N_GATES + g, nchunk, stride=blocks_per_chunk * N_GATES), :]
                  for hf in range(blocks_per_chunk)]
        return jnp.concatenate(halves, axis=1)

    for d in range(2):
        reverse = d == 1
        li = gate_rows(2 * d * M_HEADS + head)
        fp = gate_rows((2 * d + 1) * M_HEADS + head)
        lf = jnp.minimum(fp, 0.0) - jnp.log1p(jnp.exp(-jnp.abs(fp)))
        b = _scan_lanes(lf, jnp.add, 0.0, reverse)
        u = li - b
        um = _scan_lanes(u, jnp.maximum, NEG, reverse)
        u_s[d] = u
        m_prev = jnp.full((1, tc), NEG, F32)
        order = range(nchunk - 1, -1, -1) if reverse else range(nchunk)
        for c in order:
            u_max = jnp.max(u[c:c + 1, :], axis=1, keepdims=True)
            b_end = jnp.sum(lf[c:c + 1, :], axis=1, keepdims=True)
            a_end = jnp.maximum(m_prev, u_max)
            mp_s[d, c:c + 1, :] = m_prev
            al_s[d, c:c + 1, :] = a_end
            m_prev = b_end + a_end
        a = jnp.maximum(mp_s[d], um)
        a_s[d] = a
        sig_s[d] = b + a

    ones = jnp.ones((tc, LANES), BF16)
    row_i = lax.broadcasted_iota(jnp.int32, (tc, tc), 0)
    col_i = lax.broadcasted_iota(jnp.int32, (tc, tc), 1)

    def to_col(row):
        return jnp.transpose(jnp.broadcast_to(row, (LANES, tc)))

    def chunk(d, c):
        reverse = d == 1
        tok = pl.ds(pl.multiple_of(c * tc, tc), tc)
        q = q_ref[tok, :]
        kt = kt_ref[pl.ds(pl.multiple_of(c * M_QK, M_QK), M_QK), :]
        va = jnp.concatenate([v_ref[tok, :], ones], axis=1)
        u = u_s[d, pl.ds(c, 1), :]
        a_col = to_col(a_s[d, pl.ds(c, 1), :])
        sig_col = to_col(sig_s[d, pl.ds(c, 1), :])
        m_prev = mp_s[d, pl.ds(c, 1), :]
        a_end = al_s[d, pl.ds(c, 1), :]

        s = jnp.dot(q, kt.astype(BF16), preferred_element_type=F32)
        causal = (col_i >= row_i) if reverse else (col_i <= row_i)
        arg = u - jnp.concatenate([a_col] * (tc // LANES), axis=1)
        w = jnp.exp(jnp.where(causal, arg, NEG))
        intra = jnp.dot((s * w).astype(BF16), va, preferred_element_type=F32)
        c_aug = c_ref[d]
        inter = jnp.dot(q, c_aug.astype(BF16), preferred_element_type=F32)
        w_inter = jnp.exp(m_prev[:, :LANES] - a_col)
        num = jnp.concatenate([w_inter] * (M_V // LANES), axis=1) * inter[:, :M_V] + intra[:, :M_V]
        den = w_inter * inter[:, M_V:] + intra[:, M_V:]
        inv = 1.0 / jnp.maximum(jnp.abs(den), jnp.exp(-sig_col))
        h = num * jnp.concatenate([inv] * (M_V // LANES), axis=1)
        wk = jnp.exp(u - a_end)
        upd = jnp.dot((kt * wk).astype(BF16), va, preferred_element_type=F32)
        decay = jnp.exp(m_prev[:, :LANES] - a_end[:, :LANES])
        c_ref[d] = jnp.concatenate([decay] * (c_aug.shape[1] // LANES), axis=1) * c_aug + upd
        return h

    c_ref[...] = jnp.zeros(c_ref.shape, F32)

    def finalize(c, h):
        tok = pl.ds(pl.multiple_of(c * tc, tc), tc)
        hs = hbuf[tok, :] + h
        ms = jnp.mean(hs * hs, axis=-1, keepdims=True)
        y = hs * lax.rsqrt(ms + NORM_EPS) * nw_ref[...]
        y = og_ref[tok, :].astype(F32) * y * zg_ref[tok, :].astype(F32)
        y_ref[tok, :] = y.astype(BF16)

    def first_half(i, carry):
        cb = nchunk - 1 - i
        hbuf[pl.ds(pl.multiple_of(i * tc, tc), tc), :] = chunk(0, i)
        hbuf[pl.ds(pl.multiple_of(cb * tc, tc), tc), :] = chunk(1, cb)
        return carry

    def second_half(i, carry):
        cb = nchunk - 1 - i
        finalize(i, chunk(0, i))
        finalize(cb, chunk(1, cb))
        return carry

    lax.fori_loop(0, nchunk // 2, first_half, 0, unroll=2)
    lax.fori_loop(nchunk // 2, nchunk, second_half, 0, unroll=2)


def _mlstm(p, kt, gt, m_norm_w, batch, seq):
    nchunk = seq // MLSTM_CHUNK
    qm_col0 = N_MAIN_STEPS * PROJ_TN // M_QK
    vblk = lambda seg: pl.BlockSpec((seq, M_V), lambda b, h: (b, seg * (PROJ_TN // M_V) + h))
    rows = pltpu.VMEM((2, nchunk, MLSTM_CHUNK), F32)
    return pl.pallas_call(
        _mlstm_kernel,
        grid=(batch, M_HEADS),
        in_specs=[
            pl.BlockSpec((seq, M_QK), lambda b, h: (b, qm_col0 + h)),
            pl.BlockSpec((None, nchunk * M_QK, MLSTM_CHUNK), lambda b, h: (h, b, 0)),
            vblk(4), vblk(5), vblk(6),
            pl.BlockSpec((seq // LANES * N_GATES, LANES), lambda b, h: (b, 0)),
            pl.BlockSpec((1, M_V), lambda b, h: (0, h)),
        ],
        out_specs=pl.BlockSpec((seq, M_V), lambda b, h: (b, h)),
        out_shape=jax.ShapeDtypeStruct((batch * seq, M_HEADS * M_V), BF16),
        scratch_shapes=[
            pltpu.VMEM((seq, M_V), F32),
            pltpu.VMEM((2, M_QK, M_V + LANES), F32),
            rows, rows, rows, rows, rows,
        ],
        compiler_params=pltpu.CompilerParams(
            dimension_semantics=("arbitrary", "arbitrary"), vmem_limit_bytes=VMEM_LIMIT),
        name="mlstm",
    )(p, kt, p, p, p, gt, m_norm_w)


def _out_kernel(x_ref, ya_ref, ym_ref, wa_ref, wm_ref, o_ref):
    acc = jnp.dot(ya_ref[...], wa_ref[...], preferred_element_type=F32)
    acc = acc + jnp.dot(ym_ref[...], wm_ref[...], preferred_element_type=F32)
    o_ref[...] = x_ref[...] + acc


def _output_projection(x2, ya, ym, wo):
    n_tok = x2.shape[0]
    tm = OUT_TM
    one = pl.Buffered(1)
    return pl.pallas_call(
        _out_kernel,
        grid=(n_tok // tm,),
        in_specs=[
            pl.BlockSpec((tm, D_MODEL), lambda m: (m, 0)),
            pl.BlockSpec((tm, ya.shape[1]), lambda m: (m, 0)),
            pl.BlockSpec((tm, ym.shape[1]), lambda m: (m, 0)),
            pl.BlockSpec((ya.shape[1], D_MODEL), lambda m: (0, 0), pipeline_mode=one),
            pl.BlockSpec((ym.shape[1], D_MODEL), lambda m: (1, 0), pipeline_mode=one),
        ],
        out_specs=pl.BlockSpec((tm, D_MODEL), lambda m: (m, 0)),
        out_shape=jax.ShapeDtypeStruct((n_tok, D_MODEL), F32),
        compiler_params=pltpu.CompilerParams(
            dimension_semantics=("arbitrary",), vmem_limit_bytes=VMEM_LIMIT),
        name="out_proj",
    )(x2, ya, ym, wo, wo)


def _rope_tables(seq):
    pos = jnp.arange(seq, dtype=F32)
    inv_freq = ROPE_THETA ** (-jnp.arange(0, ROT_DIM, 2, dtype=F32) / ROT_DIM)
    lane = np.arange(MXU_COLS) % HEAD_DIM
    freq = jnp.where(lane < ROT_DIM, inv_freq[lane % (ROT_DIM // 2)], 0.0)
    ang = pos[:, None] * freq[None, :]
    return jnp.cos(ang), jnp.sin(ang)


def _head_matrices():
    half = ROT_DIM // 2
    src = np.arange(MXU_COLS)[:, None]
    dst = np.arange(MXU_COLS)[None, :]
    same_head = (src // HEAD_DIM) == (dst // HEAD_DIM)
    avg = np.where(same_head, 1.0 / HEAD_DIM, 0.0)
    d = dst % HEAD_DIM
    rot = (np.where(same_head & (d < half) & (src == dst + half), -1.0, 0.0)
           + np.where(same_head & (d >= half) & (d < ROT_DIM) & (src == dst - half), 1.0, 0.0))
    return jnp.asarray(avg, BF16), jnp.asarray(rot, BF16)


def _layer(x, norm_w, w_in, b_gates, q_norm_w, k_norm_w, m_norm_w, w_out, tables):
    batch, seq, _ = x.shape
    x2 = x.reshape(batch * seq, D_MODEL)
    g0 = 4 * D_ATT + 2 * M_HEADS * M_QK + 3 * M_HEADS * M_V
    w_bf = w_in.astype(BF16)
    w_g = jnp.pad(w_in[:, g0:], ((0, 0), (0, LANES - N_GATES))).astype(BF16)
    bias_bc = jnp.broadcast_to(b_gates.astype(F32).reshape(N_GATES, 1), (N_GATES, LANES))
    heads = MXU_COLS // HEAD_DIM
    p, kt, gt = _input_projection(
        x2, norm_w.reshape(1, D_MODEL), w_bf, w_g, bias_bc,
        jnp.tile(q_norm_w.reshape(1, HEAD_DIM), (1, heads)),
        jnp.tile(k_norm_w.reshape(1, HEAD_DIM), (1, heads)), *tables, seq)
    ya = _attention(p, batch, seq)
    ym = _mlstm(p, kt, gt, m_norm_w.reshape(1, M_HEADS * M_V), batch, seq)
    out = _output_projection(x2, ya, ym, w_out.astype(BF16))
    return out.reshape(x.shape)


def kernel(x, norm_w, w_in, b_gates, q_norm_w, k_norm_w, m_norm_w, w_out):
    tables = _rope_tables(x.shape[1]) + _head_matrices()
    for layer in range(norm_w.shape[0]):
        x = _layer(x, norm_w[layer], w_in[layer], b_gates[layer], q_norm_w[layer],
                   k_norm_w[layer], m_norm_w[layer], w_out[layer], tables)
    return x
```

```python
import functools
import math

import jax
import jax.numpy as jnp
import numpy as np
from jax import lax
from jax.experimental import pallas as pl
from jax.experimental.pallas import tpu as pltpu

F32 = jnp.float32
BF16 = jnp.bfloat16

D_MODEL = 2048
D_ATT = 1024
ATT_HEADS = 8
HEAD_DIM = 128
ROT_DIM = 32
ROPE_THETA = 500000.0
N_SIDE = 64
M_HEADS = 4
M_QK = 128
M_V = 256
N_GATES = 16
NORM_EPS = 1e-6
NEG = -1e30

LANES = 128
MXU_COLS = 256
MLSTM_CHUNK = 256
PROJ_TM = 1024
PROJ_TN = 1024
N_MAIN_STEPS = 7
OUT_TM = 512
VMEM_LIMIT = 56 * 1024 * 1024


def _sigmoid(x):
    return 1.0 / (1.0 + jnp.exp(-x))


_NT = (((1,), (1,)), ((), ()))


def _proj_kernel(x_ref, nw_ref, w_ref, wqm_ref, wkm_ref, wg_ref, bias_ref, qw_ref, kw_ref,
                 cos_ref, sin_ref, avg_ref, rot_ref,
                 p_ref, kt_ref, gt_ref, xn_ref):
    n = pl.program_id(1)
    tm = x_ref.shape[0]

    @pl.when(n == 0)
    def _():
        x = x_ref[...]
        ms = jnp.mean(x * x, axis=-1, keepdims=True)
        xn_ref[...] = (x * lax.rsqrt(ms + NORM_EPS) * nw_ref[...]).astype(BF16)

    def column_chunks():
        width = 2 * MXU_COLS
        for j in range(PROJ_TN // width):
            acc = lax.dot_general(xn_ref[...], w_ref[j * width:(j + 1) * width, :], _NT,
                                  preferred_element_type=F32)
            for i in range(width // MXU_COLS):
                lo = j * width + i * MXU_COLS
                yield slice(lo, lo + MXU_COLS), acc[:, i * MXU_COLS:(i + 1) * MXU_COLS]

    @pl.when(n < 2)
    def _():
        heads = MXU_COLS // HEAD_DIM
        w_row = jnp.where(n == 0, qw_ref[...] * (HEAD_DIM ** -0.5), kw_ref[...])
        cos = jnp.concatenate([cos_ref[...]] * heads, axis=1)
        sin = jnp.concatenate([sin_ref[...]] * heads, axis=1)
        for sl, acc in column_chunks():
            ms = jnp.dot((acc * acc).astype(BF16), avg_ref[...], preferred_element_type=F32)
            b = acc * w_row
            partner = jnp.dot(b.astype(BF16), rot_ref[...], preferred_element_type=F32)
            r = (b * cos + partner * sin) * lax.rsqrt(ms + NORM_EPS)
            p_ref[:, sl] = r.astype(BF16)

    @pl.when((n == 2) | (n == 4))
    def _():
        for sl, acc in column_chunks():
            p_ref[:, sl] = acc.astype(BF16)

    @pl.when((n == 3) | (n >= 5) & (n < N_MAIN_STEPS))
    def _():
        sig_only = n == 5
        for sl, acc in column_chunks():
            sg = 0.5 * jnp.tanh(0.5 * acc) + 0.5
            p_ref[:, sl] = (sg * jnp.where(sig_only, 1.0, acc)).astype(BF16)

    @pl.when(n == N_MAIN_STEPS)
    def _():
        xn = xn_ref[...]
        half = M_HEADS * M_QK
        qm = lax.dot_general(xn, wqm_ref[...], _NT, preferred_element_type=F32)
        p_ref[:, :half] = qm.astype(BF16)
        p_ref[:, half:] = jnp.zeros((tm, PROJ_TN - half), BF16)
        t = lax.dot_general(jnp.concatenate([wkm_ref[...], wg_ref[...]], axis=0), xn, _NT,
                            preferred_element_type=F32)
        for hh in range(M_HEADS):
            for j in range(tm // MLSTM_CHUNK):
                kt_ref[hh, j * M_QK:(j + 1) * M_QK, :] = (
                    t[hh * M_QK:(hh + 1) * M_QK, j * MLSTM_CHUNK:(j + 1) * MLSTM_CHUNK]
                    * (M_QK ** -0.5))
        g = t[half:, :]
        for j in range(tm // LANES):
            gt_ref[j * N_GATES:(j + 1) * N_GATES, :] = g[:, j * LANES:(j + 1) * LANES] + bias_ref[...]


def _main_weight_block(n):
    seg = jnp.minimum(n, N_MAIN_STEPS - 1)
    return seg + (seg >= 4).astype(jnp.int32)


def _input_projection(x2, norm_w, wt_bf, bias_bc, qw, kw, cos_t, sin_t, avg_m, rot_m, seq):
    n_tok = x2.shape[0]
    tm, tn = PROJ_TM, PROJ_TN
    tiles_per_seq = seq // tm
    const = lambda m, n: (0, 0)
    grid = (n_tok // tm, N_MAIN_STEPS + 1)
    one = pl.Buffered(1)
    qm_rows = M_HEADS * M_QK
    qm_block = 4 * D_ATT // qm_rows
    gate_block = (wt_bf.shape[0] - N_GATES) // N_GATES
    return pl.pallas_call(
        _proj_kernel,
        grid=grid,
        in_specs=[
            pl.BlockSpec((tm, D_MODEL), lambda m, n: (m, 0)),
            pl.BlockSpec((1, D_MODEL), const, pipeline_mode=one),
            pl.BlockSpec((tn, D_MODEL), lambda m, n: (_main_weight_block(n), 0)),
            pl.BlockSpec((qm_rows, D_MODEL), lambda m, n: (qm_block, 0), pipeline_mode=one),
            pl.BlockSpec((qm_rows, D_MODEL), lambda m, n: (qm_block + 1, 0), pipeline_mode=one),
            pl.BlockSpec((N_GATES, D_MODEL), lambda m, n: (gate_block, 0), pipeline_mode=one),
            pl.BlockSpec(bias_bc.shape, const, pipeline_mode=one),
            pl.BlockSpec((1, MXU_COLS), const, pipeline_mode=one),
            pl.BlockSpec((1, MXU_COLS), const, pipeline_mode=one),
            pl.BlockSpec((tm, HEAD_DIM), lambda m, n: (m % tiles_per_seq, 0)),
            pl.BlockSpec((tm, HEAD_DIM), lambda m, n: (m % tiles_per_seq, 0)),
            pl.BlockSpec((MXU_COLS, MXU_COLS), const, pipeline_mode=one),
            pl.BlockSpec((MXU_COLS, MXU_COLS), const, pipeline_mode=one),
        ],
        out_specs=[
            pl.BlockSpec((tm, tn), lambda m, n: (m, n)),
            pl.BlockSpec((M_HEADS, tm // MLSTM_CHUNK * M_QK, MLSTM_CHUNK), lambda m, n: (0, m, 0)),
            pl.BlockSpec((tm // LANES * N_GATES, LANES), lambda m, n: (m, 0)),
        ],
        out_shape=[
            jax.ShapeDtypeStruct((n_tok, (N_MAIN_STEPS + 1) * tn), BF16),
            jax.ShapeDtypeStruct((M_HEADS, n_tok // MLSTM_CHUNK * M_QK, MLSTM_CHUNK), F32),
            jax.ShapeDtypeStruct((n_tok // LANES * N_GATES, LANES), F32),
        ],
        scratch_shapes=[pltpu.VMEM((tm, D_MODEL), BF16)],
        compiler_params=pltpu.CompilerParams(
            dimension_semantics=("arbitrary", "arbitrary"), vmem_limit_bytes=VMEM_LIMIT),
        name="in_proj",
    )(x2, norm_w, wt_bf, wt_bf, wt_bf, wt_bf, bias_bc, qw, kw, cos_t, sin_t, avg_m, rot_m)


ATT_QT = 128
ATT_KT = ATT_QT + 2 * N_SIDE
ATT_GROUP = 8


def _band_bias(nq, nk, off):
    rel = (lax.broadcasted_iota(jnp.int32, (nq, nk), 1)
           - lax.broadcasted_iota(jnp.int32, (nq, nk), 0)) + off
    return jnp.where(jnp.abs(rel) <= N_SIDE, 0.0, NEG).astype(F32)


def _attn_kernel(q_ref, k_ref, v_ref, z_ref, o_ref, nat_f, res4_f, res4_b, st4, st1, bias_s, bias16_s):
    seq = q_ref.shape[0]
    len4 = seq // 4
    for t, ref in enumerate((q_ref, k_ref, v_ref)):
        nat_f[t] = ref[...].astype(F32)
        for r in range(4):
            x = nat_f[t, pl.ds(r, len4, stride=4), :]
            res4_f[t, r] = x
            res4_b[t, r] = x.astype(BF16)
    for j in range(3):
        bias_s[j] = _band_bias(ATT_QT, ATT_KT, -N_SIDE * j)
    nq16 = seq // 16
    bias16_s[...] = _band_bias(nq16, nq16, 0)
    ones = jnp.ones((ATT_KT, HEAD_DIM), BF16)

    def tile(q, k, v, bias, old):
        s = lax.dot_general(q, k, (((1,), (1,)), ((), ())), preferred_element_type=F32) + bias
        m_t = jnp.max(s, axis=1, keepdims=True)
        if old is None:
            m_n = jnp.broadcast_to(m_t, (q.shape[0], HEAD_DIM))
        else:
            m_o, l_o, a_o = old
            m_n = jnp.maximum(m_o, m_t)
        p = jnp.exp(s - jnp.concatenate([m_n] * (s.shape[1] // HEAD_DIM), axis=1)).astype(BF16)
        r = jnp.dot(p, jnp.concatenate([v, ones], axis=1), preferred_element_type=F32)
        a_t, l_t = r[:, :HEAD_DIM], r[:, HEAD_DIM:]
        if old is None:
            return m_n, l_t, a_t
        alpha = jnp.exp(m_o - m_n)
        return m_n, alpha * l_o + l_t, alpha * a_o + a_t

    def tile_geometry(i, length):
        kl = jnp.clip(ATT_QT * i - N_SIDE, 0, length - ATT_KT)
        return (pl.ds(pl.multiple_of(ATT_QT * i, ATT_QT), ATT_QT),
                pl.ds(pl.multiple_of(kl, N_SIDE), ATT_KT),
                (ATT_QT * i - kl) // N_SIDE)

    def body16(j, carry):
        rows = pl.ds(j, nq16, stride=4)
        for r in range(4):
            q, k, v = (res4_f[t, r, rows, :].astype(BF16) for t in range(3))
            state = tile(q, k, v, bias16_s[...], None)
            for t in range(3):
                st4[t, r, rows, :] = state[t]
        return carry

    lax.fori_loop(0, 4, body16, 0)

    def body4(ig, carry):
        for ii in range(ATT_GROUP // 4):
            i = ig * (ATT_GROUP // 4) + ii
            qs, ks, bi = tile_geometry(i, len4)
            bias = bias_s[bi]
            for r in range(4):
                state = tile(res4_b[0, r, qs, :], res4_b[1, r, ks, :], res4_b[2, r, ks, :], bias,
                             tuple(st4[t, r, qs, :] for t in range(3)))
                tok = pl.ds(r + 4 * ATT_QT * i, ATT_QT, stride=4)
                for t in range(3):
                    st1[t, tok, :] = state[t]
        return carry

    lax.fori_loop(0, len4 // ATT_QT // (ATT_GROUP // 4), body4, 0)

    def body1(ig, carry):
        for ii in range(ATT_GROUP):
            qs, ks, bi = tile_geometry(ig * ATT_GROUP + ii, seq)
            _, l, a = tile(q_ref[qs, :], k_ref[ks, :], v_ref[ks, :], bias_s[bi],
                           tuple(st1[t, qs, :] for t in range(3)))
            o_ref[qs, :] = (a / l * z_ref[qs, :].astype(F32)).astype(BF16)
        return carry

    lax.fori_loop(0, seq // ATT_QT // ATT_GROUP, body1, 0)


def _attention(p, batch, seq):
    blk = lambda col0: pl.BlockSpec((seq, HEAD_DIM), lambda b, h: (b, col0 + h))
    return pl.pallas_call(
        _attn_kernel,
        grid=(batch, ATT_HEADS),
        in_specs=[blk(0), blk(ATT_HEADS), blk(2 * ATT_HEADS), blk(3 * ATT_HEADS)],
        out_specs=pl.BlockSpec((seq, HEAD_DIM), lambda b, h: (b, h)),
        out_shape=jax.ShapeDtypeStruct((batch * seq, D_ATT), BF16),
        scratch_shapes=[
            pltpu.VMEM((3, seq, HEAD_DIM), F32),
            pltpu.VMEM((3, 4, seq // 4, HEAD_DIM), F32),
            pltpu.VMEM((3, 4, seq // 4, HEAD_DIM), BF16),
            pltpu.VMEM((3, 4, seq // 4, HEAD_DIM), F32),
            pltpu.VMEM((3, seq, HEAD_DIM), F32),
            pltpu.VMEM((3, ATT_QT, ATT_KT), F32),
            pltpu.VMEM((seq // 16, seq // 16), F32)],
        compiler_params=pltpu.CompilerParams(
            dimension_semantics=("arbitrary", "arbitrary"), vmem_limit_bytes=VMEM_LIMIT),
        name="dilated_attn",
    )(p, p, p, p)


def _scan_lanes(x, op, identity, reverse):
    width = x.shape[1]
    pos = lax.broadcasted_iota(jnp.int32, x.shape, 1)
    k = 1
    while k < width:
        if reverse:
            shifted = jnp.where(pos < width - k, pltpu.roll(x, width - k, 1), identity)
        else:
            shifted = jnp.where(pos >= k, pltpu.roll(x, k, 1), identity)
        x = op(x, shifted)
        k *= 2
    return x


def _mlstm_kernel(q_ref, kt_ref, v_ref, og_ref, zg_ref, g_ref, nw_ref, y_ref,
                  hbuf, c_ref, u_s, a_s, sig_s, mp_s, al_s):
    head = pl.program_id(1)
    seq = q_ref.shape[0]
    tc = MLSTM_CHUNK
    nchunk = seq // tc
    blocks_per_chunk = tc // LANES

    def gate_rows(g):
        halves = [g_ref[pl.ds(hf * N_GATES + g, nchunk, stride=blocks_per_chunk * N_GATES), :]
                  for hf in range(blocks_per_chunk)]
        return jnp.concatenate(halves, axis=1)

    for d in range(2):
        reverse = d == 1
        li = gate_rows(2 * d * M_HEADS + head)
        fp = gate_rows((2 * d + 1) * M_HEADS + head)
        lf = jnp.minimum(fp, 0.0) - jnp.log1p(jnp.exp(-jnp.abs(fp)))
        b = _scan_lanes(lf, jnp.add, 0.0, reverse)
        u = li - b
        um = _scan_lanes(u, jnp.maximum, NEG, reverse)
        u_s[d] = u
        m_prev = jnp.full((1, tc), NEG, F32)
        order = range(nchunk - 1, -1, -1) if reverse else range(nchunk)
        for c in order:
            u_max = jnp.max(u[c:c + 1, :], axis=1, keepdims=True)
            b_end = jnp.sum(lf[c:c + 1, :], axis=1, keepdims=True)
            a_end = jnp.maximum(m_prev, u_max)
            mp_s[d, c:c + 1, :] = m_prev
            al_s[d, c:c + 1, :] = a_end
            m_prev = b_end + a_end
        a = jnp.maximum(mp_s[d], um)
        a_s[d] = a
        sig_s[d] = b + a

    ones = jnp.ones((tc, LANES), BF16)
    row_i = lax.broadcasted_iota(jnp.int32, (tc, tc), 0)
    col_i = lax.broadcasted_iota(jnp.int32, (tc, tc), 1)

    def to_col(row):
        return jnp.transpose(jnp.broadcast_to(row, (LANES, tc)))

    def chunk(d, c):
        reverse = d == 1
        tok = pl.ds(pl.multiple_of(c * tc, tc), tc)
        q = q_ref[tok, :]
        kt = kt_ref[pl.ds(pl.multiple_of(c * M_QK, M_QK), M_QK), :]
        va = jnp.concatenate([v_ref[tok, :], ones], axis=1)
        u = u_s[d, pl.ds(c, 1), :]
        a_col = to_col(a_s[d, pl.ds(c, 1), :])
        sig_col = to_col(sig_s[d, pl.ds(c, 1), :])
        m_prev = mp_s[d, pl.ds(c, 1), :]
        a_end = al_s[d, pl.ds(c, 1), :]

        s = jnp.dot(q, kt.astype(BF16), preferred_element_type=F32)
        causal = (col_i >= row_i) if reverse else (col_i <= row_i)
        arg = u - jnp.concatenate([a_col] * (tc // LANES), axis=1)
        w = jnp.exp(jnp.where(causal, arg, NEG))
        intra = jnp.dot((s * w).astype(BF16), va, preferred_element_type=F32)
        c_aug = c_ref[d]
        inter = jnp.dot(q, c_aug.astype(BF16), preferred_element_type=F32)
        w_inter = jnp.exp(m_prev[:, :LANES] - a_col)
        num = jnp.concatenate([w_inter] * (M_V // LANES), axis=1) * inter[:, :M_V] + intra[:, :M_V]
        den = w_inter * inter[:, M_V:] + intra[:, M_V:]
        inv = 1.0 / jnp.maximum(jnp.abs(den), jnp.exp(-sig_col))
        h = num * jnp.concatenate([inv] * (M_V // LANES), axis=1)
        wk = jnp.exp(u - a_end)
        upd = jnp.dot((kt * wk).astype(BF16), va, preferred_element_type=F32)
        decay = jnp.exp(m_prev[:, :LANES] - a_end[:, :LANES])
        c_ref[d] = jnp.concatenate([decay] * (c_aug.shape[1] // LANES), axis=1) * c_aug + upd
        return h

    c_ref[...] = jnp.zeros(c_ref.shape, F32)

    def finalize(c, h):
        tok = pl.ds(pl.multiple_of(c * tc, tc), tc)
        hs = hbuf[tok, :] + h
        ms = jnp.mean(hs * hs, axis=-1, keepdims=True)
        y = hs * lax.rsqrt(ms + NORM_EPS) * nw_ref[...]
        y = og_ref[tok, :].astype(F32) * y * zg_ref[tok, :].astype(F32)
        y_ref[tok, :] = y.astype(BF16)

    def first_half(i, carry):
        cb = nchunk - 1 - i
        hbuf[pl.ds(pl.multiple_of(i * tc, tc), tc), :] = chunk(0, i)
        hbuf[pl.ds(pl.multiple_of(cb * tc, tc), tc), :] = chunk(1, cb)
        return carry

    def second_half(i, carry):
        cb = nchunk - 1 - i
        finalize(i, chunk(0, i))
        finalize(cb, chunk(1, cb))
        return carry

    lax.fori_loop(0, nchunk // 2, first_half, 0, unroll=2)
    lax.fori_loop(nchunk // 2, nchunk, second_half, 0, unroll=2)


def _mlstm(p, kt, gt, m_norm_w, batch, seq):
    nchunk = seq // MLSTM_CHUNK
    qm_col0 = N_MAIN_STEPS * PROJ_TN // M_QK
    vblk = lambda seg: pl.BlockSpec((seq, M_V), lambda b, h: (b, seg * (PROJ_TN // M_V) + h))
    rows = pltpu.VMEM((2, nchunk, MLSTM_CHUNK), F32)
    return pl.pallas_call(
        _mlstm_kernel,
        grid=(batch, M_HEADS),
        in_specs=[
            pl.BlockSpec((seq, M_QK), lambda b, h: (b, qm_col0 + h)),
            pl.BlockSpec((None, nchunk * M_QK, MLSTM_CHUNK), lambda b, h: (h, b, 0)),
            vblk(4), vblk(5), vblk(6),
            pl.BlockSpec((seq // LANES * N_GATES, LANES), lambda b, h: (b, 0)),
            pl.BlockSpec((1, M_V), lambda b, h: (0, h)),
        ],
        out_specs=pl.BlockSpec((seq, M_V), lambda b, h: (b, h)),
        out_shape=jax.ShapeDtypeStruct((batch * seq, M_HEADS * M_V), BF16),
        scratch_shapes=[
            pltpu.VMEM((seq, M_V), F32),
            pltpu.VMEM((2, M_QK, M_V + LANES), F32),
            rows, rows, rows, rows, rows,
        ],
        compiler_params=pltpu.CompilerParams(
            dimension_semantics=("arbitrary", "arbitrary"), vmem_limit_bytes=VMEM_LIMIT),
        name="mlstm",
    )(p, kt, p, p, p, gt, m_norm_w)


def _out_kernel(x_ref, ya_ref, ym_ref, wa_ref, wm_ref, o_ref):
    acc = jnp.dot(ya_ref[...], wa_ref[...], preferred_element_type=F32)
    acc = acc + jnp.dot(ym_ref[...], wm_ref[...], preferred_element_type=F32)
    o_ref[...] = x_ref[...] + acc


def _output_projection(x2, ya, ym, wo):
    n_tok = x2.shape[0]
    tm = OUT_TM
    one = pl.Buffered(1)
    return pl.pallas_call(
        _out_kernel,
        grid=(n_tok // tm,),
        in_specs=[
            pl.BlockSpec((tm, D_MODEL), lambda m: (m, 0)),
            pl.BlockSpec((tm, ya.shape[1]), lambda m: (m, 0)),
            pl.BlockSpec((tm, ym.shape[1]), lambda m: (m, 0)),
            pl.BlockSpec((ya.shape[1], D_MODEL), lambda m: (0, 0), pipeline_mode=one),
            pl.BlockSpec((ym.shape[1], D_MODEL), lambda m: (1, 0), pipeline_mode=one),
        ],
        out_specs=pl.BlockSpec((tm, D_MODEL), lambda m: (m, 0)),
        out_shape=jax.ShapeDtypeStruct((n_tok, D_MODEL), F32),
        compiler_params=pltpu.CompilerParams(
            dimension_semantics=("arbitrary",), vmem_limit_bytes=VMEM_LIMIT),
        name="out_proj",
    )(x2, ya, ym, wo, wo)


def _rope_tables(seq):
    pos = np.arange(seq, dtype=np.float64)
    inv_freq = ROPE_THETA ** (-np.arange(0, ROT_DIM, 2, dtype=np.float64) / ROT_DIM)
    lane = np.arange(HEAD_DIM)
    freq = np.where(lane < ROT_DIM, inv_freq[lane % (ROT_DIM // 2)], 0.0)
    ang = pos[:, None] * freq[None, :]
    return jnp.asarray(np.cos(ang), F32), jnp.asarray(np.sin(ang), F32)


def _head_matrices():
    half = ROT_DIM // 2
    src = np.arange(MXU_COLS)[:, None]
    dst = np.arange(MXU_COLS)[None, :]
    same_head = (src // HEAD_DIM) == (dst // HEAD_DIM)
    avg = np.where(same_head, 1.0 / HEAD_DIM, 0.0)
    d = dst % HEAD_DIM
    rot = (np.where(same_head & (d < half) & (src == dst + half), -1.0, 0.0)
           + np.where(same_head & (d >= half) & (d < ROT_DIM) & (src == dst - half), 1.0, 0.0))
    return jnp.asarray(avg, BF16), jnp.asarray(rot, BF16)


def _layer(x, norm_w, w_in, b_gates, q_norm_w, k_norm_w, m_norm_w, w_out, tables):
    batch, seq, _ = x.shape
    x2 = x.reshape(batch * seq, D_MODEL)
    wt_bf = w_in.T.astype(BF16)
    bias_bc = jnp.broadcast_to(b_gates.astype(F32).reshape(N_GATES, 1), (N_GATES, LANES))
    heads = MXU_COLS // HEAD_DIM
    p, kt, gt = _input_projection(
        x2, norm_w.reshape(1, D_MODEL), wt_bf, bias_bc,
        jnp.tile(q_norm_w.reshape(1, HEAD_DIM), (1, heads)),
        jnp.tile(k_norm_w.reshape(1, HEAD_DIM), (1, heads)), *tables, seq)
    ya = _attention(p, batch, seq)
    ym = _mlstm(p, kt, gt, m_norm_w.reshape(1, M_HEADS * M_V), batch, seq)
    out = _output_projection(x2, ya, ym, w_out.astype(BF16))
    return out.reshape(x.shape)


def kernel(x, norm_w, w_in, b_gates, q_norm_w, k_norm_w, m_norm_w, w_out):
    tables = _rope_tables(x.shape[1]) + _head_matrices()
    for layer in range(norm_w.shape[0]):
        x = _layer(x, norm_w[layer], w_in[layer], b_gates[layer], q_norm_w[layer],
                   k_norm_w[layer], m_norm_w[layer], w_out[layer], tables)
    return x
```

```python
import functools
import math

import jax
import jax.numpy as jnp
import numpy as np
from jax import lax
from jax.experimental import pallas as pl
from jax.experimental.pallas import tpu as pltpu

F32 = jnp.float32
BF16 = jnp.bfloat16

D_MODEL = 2048
D_ATT = 1024
ATT_HEADS = 8
HEAD_DIM = 128
ROT_DIM = 32
ROPE_THETA = 500000.0
N_SIDE = 64
M_HEADS = 4
M_QK = 128
M_V = 256
N_GATES = 16
NORM_EPS = 1e-6
NEG = -1e30

LANES = 128
MXU_COLS = 256
MLSTM_CHUNK = 256
PROJ_TM = 1024
PROJ_TN = 1024
N_MAIN_STEPS = 7
OUT_TM = 512
VMEM_LIMIT = 56 * 1024 * 1024


LOG2_E = math.log2(math.e)


_NT = (((1,), (1,)), ((), ()))


def _proj_kernel(x_ref, nw_ref, w_ref, wqm_ref, wkm_ref, wg_ref, bias_ref, qw_ref, kw_ref,
                 cos_ref, sin_ref, avg_ref, rot_ref,
                 p_ref, kt_ref, gt_ref, xn_ref):
    n = pl.program_id(1)
    tm = x_ref.shape[0]

    @pl.when(n == 0)
    def _():
        x = x_ref[...]
        ms = jnp.mean(x * x, axis=-1, keepdims=True)
        xn_ref[...] = (x * lax.rsqrt(ms + NORM_EPS) * nw_ref[...]).astype(BF16)

    def column_chunks():
        width = 2 * MXU_COLS
        for j in range(PROJ_TN // width):
            acc = lax.dot_general(xn_ref[...], w_ref[j * width:(j + 1) * width, :], _NT,
                                  preferred_element_type=F32)
            for i in range(width // MXU_COLS):
                lo = j * width + i * MXU_COLS
                yield slice(lo, lo + MXU_COLS), acc[:, i * MXU_COLS:(i + 1) * MXU_COLS]

    @pl.when(n < 2)
    def _():
        heads = MXU_COLS // HEAD_DIM
        w_row = jnp.where(n == 0, qw_ref[...] * (HEAD_DIM ** -0.5 * LOG2_E), kw_ref[...])
        cos = jnp.concatenate([cos_ref[...]] * heads, axis=1)
        sin = jnp.concatenate([sin_ref[...]] * heads, axis=1)
        for sl, acc in column_chunks():
            ms = jnp.dot((acc * acc).astype(BF16), avg_ref[...], preferred_element_type=F32)
            b = acc * w_row
            partner = jnp.dot(b.astype(BF16), rot_ref[...], preferred_element_type=F32)
            r = (b * cos + partner * sin) * lax.rsqrt(ms + NORM_EPS)
            p_ref[:, sl] = r.astype(BF16)

    @pl.when((n == 2) | (n == 4))
    def _():
        for sl, acc in column_chunks():
            p_ref[:, sl] = acc.astype(BF16)

    @pl.when((n == 3) | (n >= 5) & (n < N_MAIN_STEPS))
    def _():
        sig_only = n == 5
        for sl, acc in column_chunks():
            sg = 0.5 * jnp.tanh(0.5 * acc) + 0.5
            p_ref[:, sl] = (sg * jnp.where(sig_only, 1.0, acc)).astype(BF16)

    @pl.when(n == N_MAIN_STEPS)
    def _():
        xn = xn_ref[...]
        half = M_HEADS * M_QK
        qm = lax.dot_general(xn, wqm_ref[...], _NT, preferred_element_type=F32)
        p_ref[:, :half] = qm.astype(BF16)
        p_ref[:, half:] = jnp.zeros((tm, PROJ_TN - half), BF16)
        t = lax.dot_general(jnp.concatenate([wkm_ref[...], wg_ref[...]], axis=0), xn, _NT,
                            preferred_element_type=F32)
        for hh in range(M_HEADS):
            for j in range(tm // MLSTM_CHUNK):
                kt_ref[hh, j * M_QK:(j + 1) * M_QK, :] = (
                    t[hh * M_QK:(hh + 1) * M_QK, j * MLSTM_CHUNK:(j + 1) * MLSTM_CHUNK]
                    * (M_QK ** -0.5))
        g = t[half:, :]
        for j in range(tm // LANES):
            gt_ref[j * N_GATES:(j + 1) * N_GATES, :] = g[:, j * LANES:(j + 1) * LANES] + bias_ref[...]


def _main_weight_block(n):
    seg = jnp.minimum(n, N_MAIN_STEPS - 1)
    return seg + (seg >= 4).astype(jnp.int32)


def _input_projection(x2, norm_w, wt_bf, bias_bc, qw, kw, cos_t, sin_t, avg_m, rot_m, seq):
    n_tok = x2.shape[0]
    tm, tn = PROJ_TM, PROJ_TN
    tiles_per_seq = seq // tm
    const = lambda m, n: (0, 0)
    grid = (n_tok // tm, N_MAIN_STEPS + 1)
    one = pl.Buffered(1)
    qm_rows = M_HEADS * M_QK
    qm_block = 4 * D_ATT // qm_rows
    gate_block = (wt_bf.shape[0] - N_GATES) // N_GATES
    return pl.pallas_call(
        _proj_kernel,
        grid=grid,
        in_specs=[
            pl.BlockSpec((tm, D_MODEL), lambda m, n: (m, 0)),
            pl.BlockSpec((1, D_MODEL), const, pipeline_mode=one),
            pl.BlockSpec((tn, D_MODEL), lambda m, n: (_main_weight_block(n), 0)),
            pl.BlockSpec((qm_rows, D_MODEL), lambda m, n: (qm_block, 0), pipeline_mode=one),
            pl.BlockSpec((qm_rows, D_MODEL), lambda m, n: (qm_block + 1, 0), pipeline_mode=one),
            pl.BlockSpec((N_GATES, D_MODEL), lambda m, n: (gate_block, 0), pipeline_mode=one),
            pl.BlockSpec(bias_bc.shape, const, pipeline_mode=one),
            pl.BlockSpec((1, MXU_COLS), const, pipeline_mode=one),
            pl.BlockSpec((1, MXU_COLS), const, pipeline_mode=one),
            pl.BlockSpec((tm, HEAD_DIM), lambda m, n: (m % tiles_per_seq, 0)),
            pl.BlockSpec((tm, HEAD_DIM), lambda m, n: (m % tiles_per_seq, 0)),
            pl.BlockSpec((MXU_COLS, MXU_COLS), const, pipeline_mode=one),
            pl.BlockSpec((MXU_COLS, MXU_COLS), const, pipeline_mode=one),
        ],
        out_specs=[
            pl.BlockSpec((tm, tn), lambda m, n: (m, n)),
            pl.BlockSpec((M_HEADS, tm // MLSTM_CHUNK * M_QK, MLSTM_CHUNK), lambda m, n: (0, m, 0)),
            pl.BlockSpec((tm // LANES * N_GATES, LANES), lambda m, n: (m, 0)),
        ],
        out_shape=[
            jax.ShapeDtypeStruct((n_tok, (N_MAIN_STEPS + 1) * tn), BF16),
            jax.ShapeDtypeStruct((M_HEADS, n_tok // MLSTM_CHUNK * M_QK, MLSTM_CHUNK), F32),
            jax.ShapeDtypeStruct((n_tok // LANES * N_GATES, LANES), F32),
        ],
        scratch_shapes=[pltpu.VMEM((tm, D_MODEL), BF16)],
        compiler_params=pltpu.CompilerParams(
            dimension_semantics=("arbitrary", "arbitrary"), vmem_limit_bytes=VMEM_LIMIT),
        name="in_proj",
    )(x2, norm_w, wt_bf, wt_bf, wt_bf, wt_bf, bias_bc, qw, kw, cos_t, sin_t, avg_m, rot_m)


ATT_QT = 128
ATT_KT = ATT_QT + 2 * N_SIDE
ATT_GROUP = 16


def _band_bias(nq, nk, off):
    rel = (lax.broadcasted_iota(jnp.int32, (nq, nk), 1)
           - lax.broadcasted_iota(jnp.int32, (nq, nk), 0)) + off
    return jnp.where(jnp.abs(rel) <= N_SIDE, 0.0, NEG).astype(F32)


def _attn_kernel(q_ref, k_ref, v_ref, z_ref, o_ref, nat_f, res4_f, res4_b, st4, st1, bias_s, bias16_s):
    seq = q_ref.shape[0]
    len4 = seq // 4
    for t, ref in enumerate((q_ref, k_ref, v_ref)):
        nat_f[t] = ref[...].astype(F32)
        for r in range(4):
            x = nat_f[t, pl.ds(r, len4, stride=4), :]
            res4_f[t, r] = x
            res4_b[t, r] = x.astype(BF16)
    for j in range(3):
        bias_s[j] = _band_bias(ATT_QT, ATT_KT, -N_SIDE * j)
    nq16 = seq // 16
    bias16_s[...] = _band_bias(nq16, nq16, 0)
    ones = jnp.ones((ATT_KT, HEAD_DIM), BF16)

    def tile(q, k, v, bias, old):
        s = lax.dot_general(q, k, (((1,), (1,)), ((), ())), preferred_element_type=F32) + bias
        m_t = jnp.max(s, axis=1, keepdims=True)
        if old is None:
            m_n = jnp.broadcast_to(m_t, (q.shape[0], HEAD_DIM))
        else:
            m_o, l_o, a_o = old
            m_n = jnp.maximum(m_o, m_t)
        p = jnp.exp2(s - jnp.concatenate([m_n] * (s.shape[1] // HEAD_DIM), axis=1)).astype(BF16)
        r = jnp.dot(p, jnp.concatenate([v, ones], axis=1), preferred_element_type=F32)
        a_t, l_t = r[:, :HEAD_DIM], r[:, HEAD_DIM:]
        if old is None:
            return m_n, l_t, a_t
        alpha = jnp.exp2(m_o - m_n)
        return m_n, alpha * l_o + l_t, alpha * a_o + a_t

    def tile_geometry(i, length):
        kl = jnp.clip(ATT_QT * i - N_SIDE, 0, length - ATT_KT)
        return (pl.ds(pl.multiple_of(ATT_QT * i, ATT_QT), ATT_QT),
                pl.ds(pl.multiple_of(kl, N_SIDE), ATT_KT),
                (ATT_QT * i - kl) // N_SIDE)

    per_iter16 = ATT_GROUP // 8

    def body16(jg, carry):
        for jj in range(per_iter16):
            rows = pl.ds(jg * per_iter16 + jj, nq16, stride=4)
            for r in range(4):
                q, k, v = (res4_f[t, r, rows, :].astype(BF16) for t in range(3))
                state = tile(q, k, v, bias16_s[...], None)
                for t in range(3):
                    st4[t, r, rows, :] = state[t]
        return carry

    lax.fori_loop(0, 4 // per_iter16, body16, 0)

    def body4(ig, carry):
        for ii in range(ATT_GROUP // 4):
            i = ig * (ATT_GROUP // 4) + ii
            qs, ks, bi = tile_geometry(i, len4)
            bias = bias_s[bi]
            for r in range(4):
                state = tile(res4_b[0, r, qs, :], res4_b[1, r, ks, :], res4_b[2, r, ks, :], bias,
                             tuple(st4[t, r, qs, :] for t in range(3)))
                tok = pl.ds(r + 4 * ATT_QT * i, ATT_QT, stride=4)
                for t in range(3):
                    st1[t, tok, :] = state[t]
        return carry

    lax.fori_loop(0, len4 // ATT_QT // (ATT_GROUP // 4), body4, 0)

    def body1(ig, carry):
        for ii in range(ATT_GROUP):
            qs, ks, bi = tile_geometry(ig * ATT_GROUP + ii, seq)
            _, l, a = tile(q_ref[qs, :], k_ref[ks, :], v_ref[ks, :], bias_s[bi],
                           tuple(st1[t, qs, :] for t in range(3)))
            o_ref[qs, :] = (a / l * z_ref[qs, :].astype(F32)).astype(BF16)
        return carry

    lax.fori_loop(0, seq // ATT_QT // ATT_GROUP, body1, 0)


def _attention(p, batch, seq):
    blk = lambda col0: pl.BlockSpec((seq, HEAD_DIM), lambda b, h: (b, col0 + h))
    return pl.pallas_call(
        _attn_kernel,
        grid=(batch, ATT_HEADS),
        in_specs=[blk(0), blk(ATT_HEADS), blk(2 * ATT_HEADS), blk(3 * ATT_HEADS)],
        out_specs=pl.BlockSpec((seq, HEAD_DIM), lambda b, h: (b, h)),
        out_shape=jax.ShapeDtypeStruct((batch * seq, D_ATT), BF16),
        scratch_shapes=[
            pltpu.VMEM((3, seq, HEAD_DIM), F32),
            pltpu.VMEM((3, 4, seq // 4, HEAD_DIM), F32),
            pltpu.VMEM((3, 4, seq // 4, HEAD_DIM), BF16),
            pltpu.VMEM((3, 4, seq // 4, HEAD_DIM), F32),
            pltpu.VMEM((3, seq, HEAD_DIM), F32),
            pltpu.VMEM((3, ATT_QT, ATT_KT), F32),
            pltpu.VMEM((seq // 16, seq // 16), F32)],
        compiler_params=pltpu.CompilerParams(
            dimension_semantics=("arbitrary", "arbitrary"), vmem_limit_bytes=VMEM_LIMIT),
        name="dilated_attn",
    )(p, p, p, p)


def _scan_lanes(x, op, identity, reverse):
    width = x.shape[1]
    pos = lax.broadcasted_iota(jnp.int32, x.shape, 1)
    k = 1
    while k < width:
        if reverse:
            shifted = jnp.where(pos < width - k, pltpu.roll(x, width - k, 1), identity)
        else:
            shifted = jnp.where(pos >= k, pltpu.roll(x, k, 1), identity)
        x = op(x, shifted)
        k *= 2
    return x


def _mlstm_kernel(q_ref, kt_ref, v_ref, og_ref, zg_ref, g_ref, nw_ref, y_ref,
                  hbuf, c_ref, u_s, a_s, sig_s, mp_s, al_s, causal_s):
    head = pl.program_id(1)
    seq = q_ref.shape[0]
    tc = MLSTM_CHUNK
    nchunk = seq // tc
    blocks_per_chunk = tc // LANES

    def gate_rows(g):
        halves = [g_ref[pl.ds(hf * N_GATES + g, nchunk, stride=blocks_per_chunk * N_GATES), :]
                  for hf in range(blocks_per_chunk)]
        return jnp.concatenate(halves, axis=1)

    for d in range(2):
        reverse = d == 1
        li = gate_rows(2 * d * M_HEADS + head) * LOG2_E
        fp = gate_rows((2 * d + 1) * M_HEADS + head)
        lf = (jnp.minimum(fp, 0.0) - jnp.log1p(jnp.exp(-jnp.abs(fp)))) * LOG2_E
        b = _scan_lanes(lf, jnp.add, 0.0, reverse)
        u = li - b
        um = _scan_lanes(u, jnp.maximum, NEG, reverse)
        u_s[d] = u
        m_prev = jnp.full((1, tc), NEG, F32)
        order = range(nchunk - 1, -1, -1) if reverse else range(nchunk)
        for c in order:
            u_max = jnp.max(u[c:c + 1, :], axis=1, keepdims=True)
            b_end = jnp.sum(lf[c:c + 1, :], axis=1, keepdims=True)
            a_end = jnp.maximum(m_prev, u_max)
            mp_s[d, c:c + 1, :] = m_prev
            al_s[d, c:c + 1, :] = a_end
            m_prev = b_end + a_end
        a = jnp.maximum(mp_s[d], um)
        a_s[d] = a
        sig_s[d] = b + a

    ones = jnp.ones((tc, LANES), BF16)
    row_i = lax.broadcasted_iota(jnp.int32, (tc, tc), 0)
    col_i = lax.broadcasted_iota(jnp.int32, (tc, tc), 1)
    causal_s[0] = jnp.where(col_i <= row_i, 0.0, NEG).astype(F32)
    causal_s[1] = jnp.where(col_i >= row_i, 0.0, NEG).astype(F32)

    def to_col(row):
        return jnp.transpose(jnp.broadcast_to(row, (LANES, tc)))

    def chunk(d, c):
        reverse = d == 1
        tok = pl.ds(pl.multiple_of(c * tc, tc), tc)
        q = q_ref[tok, :]
        kt = kt_ref[pl.ds(pl.multiple_of(c * M_QK, M_QK), M_QK), :]
        v = v_ref[tok, :]
        va = jnp.concatenate([v, ones], axis=1)
        u = u_s[d, pl.ds(c, 1), :]
        a_col = to_col(a_s[d, pl.ds(c, 1), :])
        sig_col = to_col(sig_s[d, pl.ds(c, 1), :])
        m_prev = mp_s[d, pl.ds(c, 1), :]
        a_end = al_s[d, pl.ds(c, 1), :]

        s = jnp.dot(q, kt.astype(BF16), preferred_element_type=F32)
        arg = u - jnp.concatenate([a_col] * (tc // LANES), axis=1)
        sw = s * jnp.exp2(arg + causal_s[d])
        intra = jnp.dot(sw.astype(BF16), v, preferred_element_type=F32)
        c_aug = c_ref[d]
        inter = jnp.dot(q, c_aug.astype(BF16), preferred_element_type=F32)
        w_inter = jnp.exp2(m_prev[:, :LANES] - a_col)
        num = jnp.concatenate([w_inter] * (M_V // LANES), axis=1) * inter[:, :M_V] + intra
        den = w_inter * inter[:, M_V:] + jnp.sum(sw, axis=1, keepdims=True)
        inv = 1.0 / jnp.maximum(jnp.abs(den), jnp.exp2(-sig_col))
        h = num * jnp.concatenate([inv] * (M_V // LANES), axis=1)
        wk = jnp.exp2(u - a_end)
        upd = jnp.dot((kt * wk).astype(BF16), va, preferred_element_type=F32)
        decay = jnp.exp2(m_prev[:, :LANES] - a_end[:, :LANES])
        c_ref[d] = jnp.concatenate([decay] * (c_aug.shape[1] // LANES), axis=1) * c_aug + upd
        return h

    c_ref[...] = jnp.zeros(c_ref.shape, F32)

    def finalize(c, h):
        tok = pl.ds(pl.multiple_of(c * tc, tc), tc)
        hs = hbuf[tok, :] + h
        ms = jnp.mean(hs * hs, axis=-1, keepdims=True)
        y = hs * lax.rsqrt(ms + NORM_EPS) * nw_ref[...]
        y = og_ref[tok, :].astype(F32) * y * zg_ref[tok, :].astype(F32)
        y_ref[tok, :] = y.astype(BF16)

    def first_half(i, carry):
        cb = nchunk - 1 - i
        hbuf[pl.ds(pl.multiple_of(i * tc, tc), tc), :] = chunk(0, i)
        hbuf[pl.ds(pl.multiple_of(cb * tc, tc), tc), :] = chunk(1, cb)
        return carry

    def second_half(i, carry):
        cb = nchunk - 1 - i
        finalize(i, chunk(0, i))
        finalize(cb, chunk(1, cb))
        return carry

    lax.fori_loop(0, nchunk // 2, first_half, 0, unroll=4)
    lax.fori_loop(nchunk // 2, nchunk, second_half, 0, unroll=4)


def _mlstm(p, kt, gt, m_norm_w, batch, seq):
    nchunk = seq // MLSTM_CHUNK
    qm_col0 = N_MAIN_STEPS * PROJ_TN // M_QK
    vblk = lambda seg: pl.BlockSpec((seq, M_V), lambda b, h: (b, seg * (PROJ_TN // M_V) + h))
    rows = pltpu.VMEM((2, nchunk, MLSTM_CHUNK), F32)
    return pl.pallas_call(
        _mlstm_kernel,
        grid=(batch, M_HEADS),
        in_specs=[
            pl.BlockSpec((seq, M_QK), lambda b, h: (b, qm_col0 + h)),
            pl.BlockSpec((None, nchunk * M_QK, MLSTM_CHUNK), lambda b, h: (h, b, 0)),
            vblk(4), vblk(5), vblk(6),
            pl.BlockSpec((seq // LANES * N_GATES, LANES), lambda b, h: (b, 0)),
            pl.BlockSpec((1, M_V), lambda b, h: (0, h)),
        ],
        out_specs=pl.BlockSpec((seq, M_V), lambda b, h: (b, h)),
        out_shape=jax.ShapeDtypeStruct((batch * seq, M_HEADS * M_V), BF16),
        scratch_shapes=[
            pltpu.VMEM((seq, M_V), F32),
            pltpu.VMEM((2, M_QK, M_V + LANES), F32),
            rows, rows, rows, rows, rows,
            pltpu.VMEM((2, MLSTM_CHUNK, MLSTM_CHUNK), F32),
        ],
        compiler_params=pltpu.CompilerParams(
            dimension_semantics=("arbitrary", "arbitrary"), vmem_limit_bytes=VMEM_LIMIT),
        name="mlstm",
    )(p, kt, p, p, p, gt, m_norm_w)


def _out_kernel(x_ref, ya_ref, ym_ref, wa_ref, wm_ref, o_ref):
    acc = jnp.dot(ya_ref[...], wa_ref[...], preferred_element_type=F32)
    acc = acc + jnp.dot(ym_ref[...], wm_ref[...], preferred_element_type=F32)
    o_ref[...] = x_ref[...] + acc


def _output_projection(x2, ya, ym, wo):
    n_tok = x2.shape[0]
    tm = OUT_TM
    one = pl.Buffered(1)
    return pl.pallas_call(
        _out_kernel,
        grid=(n_tok // tm,),
        in_specs=[
            pl.BlockSpec((tm, D_MODEL), lambda m: (m, 0)),
            pl.BlockSpec((tm, ya.shape[1]), lambda m: (m, 0)),
            pl.BlockSpec((tm, ym.shape[1]), lambda m: (m, 0)),
            pl.BlockSpec((ya.shape[1], D_MODEL), lambda m: (0, 0), pipeline_mode=one),
            pl.BlockSpec((ym.shape[1], D_MODEL), lambda m: (1, 0), pipeline_mode=one),
        ],
        out_specs=pl.BlockSpec((tm, D_MODEL), lambda m: (m, 0)),
        out_shape=jax.ShapeDtypeStruct((n_tok, D_MODEL), F32),
        compiler_params=pltpu.CompilerParams(
            dimension_semantics=("arbitrary",), vmem_limit_bytes=VMEM_LIMIT),
        name="out_proj",
    )(x2, ya, ym, wo, wo)


def _rope_tables(seq):
    pos = np.arange(seq, dtype=np.float64)
    inv_freq = ROPE_THETA ** (-np.arange(0, ROT_DIM, 2, dtype=np.float64) / ROT_DIM)
    lane = np.arange(HEAD_DIM)
    freq = np.where(lane < ROT_DIM, inv_freq[lane % (ROT_DIM // 2)], 0.0)
    ang = pos[:, None] * freq[None, :]
    return jnp.asarray(np.cos(ang), F32), jnp.asarray(np.sin(ang), F32)


def _head_matrices():
    half = ROT_DIM // 2
    src = np.arange(MXU_COLS)[:, None]
    dst = np.arange(MXU_COLS)[None, :]
    same_head = (src // HEAD_DIM) == (dst // HEAD_DIM)
    avg = np.where(same_head, 1.0 / HEAD_DIM, 0.0)
    d = dst % HEAD_DIM
    rot = (np.where(same_head & (d < half) & (src == dst + half), -1.0, 0.0)
           + np.where(same_head & (d >= half) & (d < ROT_DIM) & (src == dst - half), 1.0, 0.0))
    return jnp.asarray(avg, BF16), jnp.asarray(rot, BF16)


def _layer(x, norm_w, w_in, b_gates, q_norm_w, k_norm_w, m_norm_w, w_out, tables):
    batch, seq, _ = x.shape
    x2 = x.reshape(batch * seq, D_MODEL)
    wt_bf = w_in.T.astype(BF16)
    bias_bc = jnp.broadcast_to(b_gates.astype(F32).reshape(N_GATES, 1), (N_GATES, LANES))
    heads = MXU_COLS // HEAD_DIM
    p, kt, gt = _input_projection(
        x2, norm_w.reshape(1, D_MODEL), wt_bf, bias_bc,
        jnp.tile(q_norm_w.reshape(1, HEAD_DIM), (1, heads)),
        jnp.tile(k_norm_w.reshape(1, HEAD_DIM), (1, heads)), *tables, seq)
    ya = _attention(p, batch, seq)
    ym = _mlstm(p, kt, gt, m_norm_w.reshape(1, M_HEADS * M_V), batch, seq)
    out = _output_projection(x2, ya, ym, w_out.astype(BF16))
    return out.reshape(x.shape)


def kernel(x, norm_w, w_in, b_gates, q_norm_w, k_norm_w, m_norm_w, w_out):
    tables = _rope_tables(x.shape[1]) + _head_matrices()
    for layer in range(norm_w.shape[0]):
        x = _layer(x, norm_w[layer], w_in[layer], b_gates[layer], q_norm_w[layer],
                   k_norm_w[layer], m_norm_w[layer], w_out[layer], tables)
    return x
```

```python
import functools
import math

import jax
import jax.numpy as jnp
import numpy as np
from jax import lax
from jax.experimental import pallas as pl
from jax.experimental.pallas import tpu as pltpu

F32 = jnp.float32
BF16 = jnp.bfloat16

D_MODEL = 2048
D_ATT = 1024
ATT_HEADS = 8
HEAD_DIM = 128
ROT_DIM = 32
ROPE_THETA = 500000.0
N_SIDE = 64
M_HEADS = 4
M_QK = 128
M_V = 256
N_GATES = 16
NORM_EPS = 1e-6
NEG = -1e30

LANES = 128
MXU_COLS = 256
MLSTM_CHUNK = 256
PROJ_TM = 1024
PROJ_TN = 1024
PROJ_CHUNKS = (512, 512)
N_MAIN_STEPS = 7
OUT_TM = 512
VMEM_LIMIT = 56 * 1024 * 1024


LOG2_E = math.log2(math.e)


_NT = (((1,), (1,)), ((), ()))


def _proj_kernel(x_ref, nw_ref, w_ref, wqm_ref, wkm_ref, wg_ref, bias_ref, qw_ref, kw_ref,
                 cos_ref, sin_ref, rot_ref,
                 p_ref, kt_ref, gt_ref, xn_ref):
    n = pl.program_id(1)
    tm = x_ref.shape[0]

    @pl.when(n == 0)
    def _():
        x = x_ref[...]
        ms = jnp.mean(x * x, axis=-1, keepdims=True)
        xn_ref[...] = (x * lax.rsqrt(ms + NORM_EPS) * nw_ref[...]).astype(BF16)

    def column_chunks():
        lo = 0
        for width in PROJ_CHUNKS:
            acc = lax.dot_general(xn_ref[...], w_ref[lo:lo + width, :], _NT,
                                  preferred_element_type=F32)
            for i in range(width // MXU_COLS):
                yield (slice(lo + i * MXU_COLS, lo + (i + 1) * MXU_COLS),
                       acc[:, i * MXU_COLS:(i + 1) * MXU_COLS])
            lo += width

    @pl.when(n < 2)
    def _():
        heads = MXU_COLS // HEAD_DIM
        w_row = jnp.where(n == 0, qw_ref[...] * (HEAD_DIM ** -0.5 * LOG2_E), kw_ref[...])
        cos = jnp.concatenate([cos_ref[...]] * heads, axis=1)
        sin = jnp.concatenate([sin_ref[...]] * heads, axis=1)
        for sl, acc in column_chunks():
            sq = acc * acc
            ms = jnp.concatenate(
                [jnp.broadcast_to(jnp.mean(sq[:, h * HEAD_DIM:(h + 1) * HEAD_DIM], axis=1,
                                           keepdims=True), (tm, HEAD_DIM)) for h in range(heads)],
                axis=1)
            b = acc * w_row
            partner = jnp.dot(b.astype(BF16), rot_ref[...], preferred_element_type=F32)
            r = (b * cos + partner * sin) * lax.rsqrt(ms + NORM_EPS)
            p_ref[:, sl] = r.astype(BF16)

    @pl.when((n == 2) | (n == 4))
    def _():
        for sl, acc in column_chunks():
            p_ref[:, sl] = acc.astype(BF16)

    @pl.when((n == 3) | (n >= 5) & (n < N_MAIN_STEPS))
    def _():
        sig_only = n == 5
        for sl, acc in column_chunks():
            sg = 0.5 * jnp.tanh(0.5 * acc) + 0.5
            p_ref[:, sl] = (sg * jnp.where(sig_only, 1.0, acc)).astype(BF16)

    @pl.when(n == N_MAIN_STEPS)
    def _():
        xn = xn_ref[...]
        half = M_HEADS * M_QK
        qm = lax.dot_general(xn, wqm_ref[...], _NT, preferred_element_type=F32)
        p_ref[:, :half] = qm.astype(BF16)
        p_ref[:, half:] = jnp.zeros((tm, PROJ_TN - half), BF16)
        t = lax.dot_general(jnp.concatenate([wkm_ref[...], wg_ref[...]], axis=0), xn, _NT,
                            preferred_element_type=F32)
        for hh in range(M_HEADS):
            for j in range(tm // MLSTM_CHUNK):
                kt_ref[hh, j * M_QK:(j + 1) * M_QK, :] = (
                    t[hh * M_QK:(hh + 1) * M_QK, j * MLSTM_CHUNK:(j + 1) * MLSTM_CHUNK]
                    * (M_QK ** -0.5))
        g = t[half:, :]
        for j in range(tm // LANES):
            gt_ref[j * N_GATES:(j + 1) * N_GATES, :] = g[:, j * LANES:(j + 1) * LANES] + bias_ref[...]


def _main_weight_block(n):
    seg = jnp.minimum(n, N_MAIN_STEPS - 1)
    return seg + (seg >= 4).astype(jnp.int32)


def _input_projection(x2, norm_w, wt_bf, bias_bc, qw, kw, cos_t, sin_t, rot_m, seq):
    n_tok = x2.shape[0]
    tm, tn = PROJ_TM, PROJ_TN
    tiles_per_seq = seq // tm
    const = lambda m, n: (0, 0)
    grid = (n_tok // tm, N_MAIN_STEPS + 1)
    one = pl.Buffered(1)
    qm_rows = M_HEADS * M_QK
    qm_block = 4 * D_ATT // qm_rows
    gate_block = (wt_bf.shape[0] - N_GATES) // N_GATES
    return pl.pallas_call(
        _proj_kernel,
        grid=grid,
        in_specs=[
            pl.BlockSpec((tm, D_MODEL), lambda m, n: (m, 0)),
            pl.BlockSpec((1, D_MODEL), const, pipeline_mode=one),
            pl.BlockSpec((tn, D_MODEL), lambda m, n: (_main_weight_block(n), 0)),
            pl.BlockSpec((qm_rows, D_MODEL), lambda m, n: (qm_block, 0), pipeline_mode=one),
            pl.BlockSpec((qm_rows, D_MODEL), lambda m, n: (qm_block + 1, 0), pipeline_mode=one),
            pl.BlockSpec((N_GATES, D_MODEL), lambda m, n: (gate_block, 0), pipeline_mode=one),
            pl.BlockSpec(bias_bc.shape, const, pipeline_mode=one),
            pl.BlockSpec((1, MXU_COLS), const, pipeline_mode=one),
            pl.BlockSpec((1, MXU_COLS), const, pipeline_mode=one),
            pl.BlockSpec((tm, HEAD_DIM), lambda m, n: (m % tiles_per_seq, 0)),
            pl.BlockSpec((tm, HEAD_DIM), lambda m, n: (m % tiles_per_seq, 0)),
            pl.BlockSpec((MXU_COLS, MXU_COLS), const, pipeline_mode=one),
        ],
        out_specs=[
            pl.BlockSpec((tm, tn), lambda m, n: (m, n)),
            pl.BlockSpec((M_HEADS, tm // MLSTM_CHUNK * M_QK, MLSTM_CHUNK), lambda m, n: (0, m, 0)),
            pl.BlockSpec((tm // LANES * N_GATES, LANES), lambda m, n: (m, 0)),
        ],
        out_shape=[
            jax.ShapeDtypeStruct((n_tok, (N_MAIN_STEPS + 1) * tn), BF16),
            jax.ShapeDtypeStruct((M_HEADS, n_tok // MLSTM_CHUNK * M_QK, MLSTM_CHUNK), F32),
            jax.ShapeDtypeStruct((n_tok // LANES * N_GATES, LANES), F32),
        ],
        scratch_shapes=[pltpu.VMEM((tm, D_MODEL), BF16)],
        compiler_params=pltpu.CompilerParams(
            dimension_semantics=("arbitrary", "arbitrary"), vmem_limit_bytes=VMEM_LIMIT),
        name="in_proj",
    )(x2, norm_w, wt_bf, wt_bf, wt_bf, wt_bf, bias_bc, qw, kw, cos_t, sin_t, rot_m)


ATT_QT = 128
ATT_KT = ATT_QT + 2 * N_SIDE
ATT_GROUP = 16


def _band_bias(nq, nk, off):
    rel = (lax.broadcasted_iota(jnp.int32, (nq, nk), 1)
           - lax.broadcasted_iota(jnp.int32, (nq, nk), 0)) + off
    return jnp.where(jnp.abs(rel) <= N_SIDE, 0.0, NEG).astype(F32)


def _attn_kernel(q_ref, k_ref, v_ref, z_ref, o_ref, nat_f, res4_f, st4, st1, bias_s, bias16_s):
    seq = q_ref.shape[0]
    len4 = seq // 4
    for t, ref in enumerate((q_ref, k_ref, v_ref)):
        nat_f[t] = ref[...].astype(F32)
        for r in range(4):
            res4_f[t, r] = nat_f[t, pl.ds(r, len4, stride=4), :]
    nq16 = seq // 16

    @pl.when((pl.program_id(0) == 0) & (pl.program_id(1) == 0))
    def _():
        for j in range(3):
            bias_s[j] = _band_bias(ATT_QT, ATT_KT, -N_SIDE * j)
        bias16_s[...] = _band_bias(nq16, nq16, 0)

    ones = jnp.ones((ATT_KT, HEAD_DIM), BF16)

    def tile(q, k, v, bias, old):
        s = lax.dot_general(q, k, (((1,), (1,)), ((), ())), preferred_element_type=F32) + bias
        m_t = jnp.max(s, axis=1, keepdims=True)
        if old is None:
            m_n = jnp.broadcast_to(m_t, (q.shape[0], HEAD_DIM))
        else:
            m_o, l_o, a_o = old
            m_n = jnp.maximum(m_o, m_t)
        p = jnp.exp2(s - jnp.concatenate([m_n] * (s.shape[1] // HEAD_DIM), axis=1)).astype(BF16)
        r = jnp.dot(p, jnp.concatenate([v, ones], axis=1), preferred_element_type=F32)
        a_t, l_t = r[:, :HEAD_DIM], r[:, HEAD_DIM:]
        if old is None:
            return m_n, l_t, a_t
        alpha = jnp.exp2(m_o - m_n)
        return m_n, alpha * l_o + l_t, alpha * a_o + a_t

    def tile_geometry(i, length):
        kl = jnp.clip(ATT_QT * i - N_SIDE, 0, length - ATT_KT)
        return (pl.ds(pl.multiple_of(ATT_QT * i, ATT_QT), ATT_QT),
                pl.ds(pl.multiple_of(kl, N_SIDE), ATT_KT),
                (ATT_QT * i - kl) // N_SIDE)

    per_iter16 = ATT_GROUP // 8

    def body16(jg, carry):
        for jj in range(per_iter16):
            rows = pl.ds(jg * per_iter16 + jj, nq16, stride=4)
            for r in range(4):
                q, k, v = (res4_f[t, r, rows, :].astype(BF16) for t in range(3))
                state = tile(q, k, v, bias16_s[...], None)
                for t in range(3):
                    st4[t, r, rows, :] = state[t]
        return carry

    lax.fori_loop(0, 4 // per_iter16, body16, 0)

    def body4(ig, carry):
        for ii in range(ATT_GROUP // 4):
            i = ig * (ATT_GROUP // 4) + ii
            qs, ks, bi = tile_geometry(i, len4)
            bias = bias_s[bi]
            for r in range(4):
                state = tile(res4_f[0, r, qs, :].astype(BF16), res4_f[1, r, ks, :].astype(BF16),
                             res4_f[2, r, ks, :].astype(BF16), bias,
                             tuple(st4[t, r, qs, :] for t in range(3)))
                tok = pl.ds(r + 4 * ATT_QT * i, ATT_QT, stride=4)
                for t in range(3):
                    st1[t, tok, :] = state[t]
        return carry

    lax.fori_loop(0, len4 // ATT_QT // (ATT_GROUP // 4), body4, 0)

    def body1(ig, carry):
        for ii in range(ATT_GROUP):
            qs, ks, bi = tile_geometry(ig * ATT_GROUP + ii, seq)
            _, l, a = tile(q_ref[qs, :], k_ref[ks, :], v_ref[ks, :], bias_s[bi],
                           tuple(st1[t, qs, :] for t in range(3)))
            o_ref[qs, :] = (a / l * z_ref[qs, :].astype(F32)).astype(BF16)
        return carry

    lax.fori_loop(0, seq // ATT_QT // ATT_GROUP, body1, 0)


def _attention(p, batch, seq):
    blk = lambda col0: pl.BlockSpec((seq, HEAD_DIM), lambda b, h: (b, col0 + h))
    return pl.pallas_call(
        _attn_kernel,
        grid=(batch, ATT_HEADS),
        in_specs=[blk(0), blk(ATT_HEADS), blk(2 * ATT_HEADS), blk(3 * ATT_HEADS)],
        out_specs=pl.BlockSpec((seq, HEAD_DIM), lambda b, h: (b, h)),
        out_shape=jax.ShapeDtypeStruct((batch * seq, D_ATT), BF16),
        scratch_shapes=[
            pltpu.VMEM((3, seq, HEAD_DIM), F32),
            pltpu.VMEM((3, 4, seq // 4, HEAD_DIM), F32),
            pltpu.VMEM((3, 4, seq // 4, HEAD_DIM), F32),
            pltpu.VMEM((3, seq, HEAD_DIM), F32),
            pltpu.VMEM((3, ATT_QT, ATT_KT), F32),
            pltpu.VMEM((seq // 16, seq // 16), F32)],
        compiler_params=pltpu.CompilerParams(
            dimension_semantics=("arbitrary", "arbitrary"), vmem_limit_bytes=VMEM_LIMIT),
        name="dilated_attn",
    )(p, p, p, p)


def _scan_lanes(x, op, identity, reverse):
    width = x.shape[1]
    pos = lax.broadcasted_iota(jnp.int32, x.shape, 1)
    k = 1
    while k < width:
        if reverse:
            shifted = jnp.where(pos < width - k, pltpu.roll(x, width - k, 1), identity)
        else:
            shifted = jnp.where(pos >= k, pltpu.roll(x, k, 1), identity)
        x = op(x, shifted)
        k *= 2
    return x


def _mlstm_kernel(q_ref, kt_ref, v_ref, og_ref, zg_ref, g_ref, nw_ref, y_ref,
                  hbuf, c_ref, u_s, a_s, sig_s, mp_s, al_s, causal_s):
    head = pl.program_id(1)
    seq = q_ref.shape[0]
    tc = MLSTM_CHUNK
    nchunk = seq // tc
    blocks_per_chunk = tc // LANES

    def gate_rows(g):
        halves = [g_ref[pl.ds(hf * N_GATES + g, nchunk, stride=blocks_per_chunk * N_GATES), :]
                  for hf in range(blocks_per_chunk)]
        return jnp.concatenate(halves, axis=1)

    ones = jnp.ones((tc, LANES), BF16)

    def prepare(d, h):
        reverse = d == 1
        ch = d * M_HEADS + h
        li = gate_rows(2 * d * M_HEADS + h) * LOG2_E
        fp = gate_rows((2 * d + 1) * M_HEADS + h)
        lf = (jnp.minimum(fp, 0.0) - jnp.log1p(jnp.exp(-jnp.abs(fp)))) * LOG2_E
        b = _scan_lanes(lf, jnp.add, 0.0, reverse)
        u = li - b
        um = _scan_lanes(u, jnp.maximum, NEG, reverse)
        u_s[ch] = u
        m_prev = jnp.full((1, tc), NEG, F32)
        order = range(nchunk - 1, -1, -1) if reverse else range(nchunk)
        for c in order:
            u_max = jnp.max(u[c:c + 1, :], axis=1, keepdims=True)
            b_end = jnp.sum(lf[c:c + 1, :], axis=1, keepdims=True)
            a_end = jnp.maximum(m_prev, u_max)
            mp_s[ch, c:c + 1, :] = m_prev
            al_s[ch, c:c + 1, :] = a_end
            m_prev = b_end + a_end
        a = jnp.maximum(mp_s[ch], um)
        a_s[ch] = a
        sig_s[ch] = b + a

    @pl.when(head == 0)
    def _():
        for d in range(2):
            for h in range(M_HEADS):
                prepare(d, h)
        row_i = lax.broadcasted_iota(jnp.int32, (tc, tc), 0)
        col_i = lax.broadcasted_iota(jnp.int32, (tc, tc), 1)
        causal_s[0] = jnp.where(col_i <= row_i, 0.0, NEG).astype(F32)
        causal_s[1] = jnp.where(col_i >= row_i, 0.0, NEG).astype(F32)

    def to_col(row):
        return jnp.transpose(jnp.broadcast_to(row, (LANES, tc)))

    def chunk(d, c):
        reverse = d == 1
        tok = pl.ds(pl.multiple_of(c * tc, tc), tc)
        q = q_ref[tok, :]
        kt = kt_ref[pl.ds(pl.multiple_of(c * M_QK, M_QK), M_QK), :]
        v = v_ref[tok, :]
        va = jnp.concatenate([v, ones], axis=1)
        ch = d * M_HEADS + head
        u = u_s[ch, pl.ds(c, 1), :]
        a_col = to_col(a_s[ch, pl.ds(c, 1), :])
        sig_col = to_col(sig_s[ch, pl.ds(c, 1), :])
        m_prev = mp_s[ch, pl.ds(c, 1), :]
        a_end = al_s[ch, pl.ds(c, 1), :]

        s = jnp.dot(q, kt.astype(BF16), preferred_element_type=F32)
        arg = u - jnp.concatenate([a_col] * (tc // LANES), axis=1)
        sw = s * jnp.exp2(arg + causal_s[d])
        intra = jnp.dot(sw.astype(BF16), v, preferred_element_type=F32)
        c_aug = c_ref[d]
        inter = jnp.dot(q, c_aug.astype(BF16), preferred_element_type=F32)
        w_inter = jnp.exp2(m_prev[:, :LANES] - a_col)
        num = jnp.concatenate([w_inter] * (M_V // LANES), axis=1) * inter[:, :M_V] + intra
        den = w_inter * inter[:, M_V:] + jnp.sum(sw, axis=1, keepdims=True)
        inv = 1.0 / jnp.maximum(jnp.abs(den), jnp.exp2(-sig_col))
        h = num * jnp.concatenate([inv] * (M_V // LANES), axis=1)
        wk = jnp.exp2(u - a_end)
        upd = jnp.dot((kt * wk).astype(BF16), va, preferred_element_type=F32)
        decay = jnp.exp2(m_prev[:, :LANES] - a_end[:, :LANES])
        c_ref[d] = jnp.concatenate([decay] * (c_aug.shape[1] // LANES), axis=1) * c_aug + upd
        return h

    c_ref[...] = jnp.zeros(c_ref.shape, F32)

    def finalize(c, h):
        tok = pl.ds(pl.multiple_of(c * tc, tc), tc)
        hs = hbuf[tok, :] + h
        ms = jnp.mean(hs * hs, axis=-1, keepdims=True)
        y = hs * lax.rsqrt(ms + NORM_EPS) * nw_ref[...]
        y = og_ref[tok, :].astype(F32) * y * zg_ref[tok, :].astype(F32)
        y_ref[tok, :] = y.astype(BF16)

    def first_half(i, carry):
        cb = nchunk - 1 - i
        hbuf[pl.ds(pl.multiple_of(i * tc, tc), tc), :] = chunk(0, i)
        hbuf[pl.ds(pl.multiple_of(cb * tc, tc), tc), :] = chunk(1, cb)
        return carry

    def second_half(i, carry):
        cb = nchunk - 1 - i
        finalize(i, chunk(0, i))
        finalize(cb, chunk(1, cb))
        return carry

    lax.fori_loop(0, nchunk // 2, first_half, 0, unroll=4)
    lax.fori_loop(nchunk // 2, nchunk, second_half, 0, unroll=4)


def _mlstm(p, kt, gt, m_norm_w, batch, seq):
    nchunk = seq // MLSTM_CHUNK
    qm_col0 = N_MAIN_STEPS * PROJ_TN // M_QK
    vblk = lambda seg: pl.BlockSpec((seq, M_V), lambda b, h: (b, seg * (PROJ_TN // M_V) + h))
    rows = pltpu.VMEM((2 * M_HEADS, nchunk, MLSTM_CHUNK), F32)
    return pl.pallas_call(
        _mlstm_kernel,
        grid=(batch, M_HEADS),
        in_specs=[
            pl.BlockSpec((seq, M_QK), lambda b, h: (b, qm_col0 + h)),
            pl.BlockSpec((None, nchunk * M_QK, MLSTM_CHUNK), lambda b, h: (h, b, 0)),
            vblk(4), vblk(5), vblk(6),
            pl.BlockSpec((seq // LANES * N_GATES, LANES), lambda b, h: (b, 0)),
            pl.BlockSpec((1, M_V), lambda b, h: (0, h)),
        ],
        out_specs=pl.BlockSpec((seq, M_V), lambda b, h: (b, h)),
        out_shape=jax.ShapeDtypeStruct((batch * seq, M_HEADS * M_V), BF16),
        scratch_shapes=[
            pltpu.VMEM((seq, M_V), F32),
            pltpu.VMEM((2, M_QK, M_V + LANES), F32),
            rows, rows, rows, rows, rows,
            pltpu.VMEM((2, MLSTM_CHUNK, MLSTM_CHUNK), F32),
        ],
        compiler_params=pltpu.CompilerParams(
            dimension_semantics=("arbitrary", "arbitrary"), vmem_limit_bytes=VMEM_LIMIT),
        name="mlstm",
    )(p, kt, p, p, p, gt, m_norm_w)


def _out_kernel(x_ref, ya_ref, ym_ref, wa_ref, wm_ref, o_ref):
    acc = jnp.dot(ya_ref[...], wa_ref[...], preferred_element_type=F32)
    acc = acc + jnp.dot(ym_ref[...], wm_ref[...], preferred_element_type=F32)
    o_ref[...] = x_ref[...] + acc


def _output_projection(x2, ya, ym, wo):
    n_tok = x2.shape[0]
    tm = OUT_TM
    one = pl.Buffered(1)
    return pl.pallas_call(
        _out_kernel,
        grid=(n_tok // tm,),
        in_specs=[
            pl.BlockSpec((tm, D_MODEL), lambda m: (m, 0)),
            pl.BlockSpec((tm, ya.shape[1]), lambda m: (m, 0)),
            pl.BlockSpec((tm, ym.shape[1]), lambda m: (m, 0)),
            pl.BlockSpec((ya.shape[1], D_MODEL), lambda m: (0, 0), pipeline_mode=one),
            pl.BlockSpec((ym.shape[1], D_MODEL), lambda m: (1, 0), pipeline_mode=one),
        ],
        out_specs=pl.BlockSpec((tm, D_MODEL), lambda m: (m, 0)),
        out_shape=jax.ShapeDtypeStruct((n_tok, D_MODEL), F32),
        compiler_params=pltpu.CompilerParams(
            dimension_semantics=("arbitrary",), vmem_limit_bytes=VMEM_LIMIT),
        name="out_proj",
    )(x2, ya, ym, wo, wo)


def _rope_tables(seq):
    pos = np.arange(seq, dtype=np.float64)
    inv_freq = ROPE_THETA ** (-np.arange(0, ROT_DIM, 2, dtype=np.float64) / ROT_DIM)
    lane = np.arange(HEAD_DIM)
    freq = np.where(lane < ROT_DIM, inv_freq[lane % (ROT_DIM // 2)], 0.0)
    ang = pos[:, None] * freq[None, :]
    return jnp.asarray(np.cos(ang), F32), jnp.asarray(np.sin(ang), F32)


def _rotate_half_matrix():
    half = ROT_DIM // 2
    src = np.arange(MXU_COLS)[:, None]
    dst = np.arange(MXU_COLS)[None, :]
    same_head = (src // HEAD_DIM) == (dst // HEAD_DIM)
    d = dst % HEAD_DIM
    rot = (np.where(same_head & (d < half) & (src == dst + half), -1.0, 0.0)
           + np.where(same_head & (d >= half) & (d < ROT_DIM) & (src == dst - half), 1.0, 0.0))
    return (jnp.asarray(rot, BF16),)


def _layer(x, norm_w, w_in, b_gates, q_norm_w, k_norm_w, m_norm_w, w_out, tables):
    batch, seq, _ = x.shape
    x2 = x.reshape(batch * seq, D_MODEL)
    wt_bf = w_in.T.astype(BF16)
    bias_bc = jnp.broadcast_to(b_gates.astype(F32).reshape(N_GATES, 1), (N_GATES, LANES))
    heads = MXU_COLS // HEAD_DIM
    p, kt, gt = _input_projection(
        x2, norm_w.reshape(1, D_MODEL), wt_bf, bias_bc,
        jnp.tile(q_norm_w.reshape(1, HEAD_DIM), (1, heads)),
        jnp.tile(k_norm_w.reshape(1, HEAD_DIM), (1, heads)), *tables, seq)
    ya = _attention(p, batch, seq)
    ym = _mlstm(p, kt, gt, m_norm_w.reshape(1, M_HEADS * M_V), batch, seq)
    out = _output_projection(x2, ya, ym, w_out.astype(BF16))
    return out.reshape(x.shape)


def kernel(x, norm_w, w_in, b_gates, q_norm_w, k_norm_w, m_norm_w, w_out):
    tables = _rope_tables(x.shape[1]) + _rotate_half_matrix()
    for layer in range(norm_w.shape[0]):
        x = _layer(x, norm_w[layer], w_in[layer], b_gates[layer], q_norm_w[layer],
                   k_norm_w[layer], m_norm_w[layer], w_out[layer], tables)
    return x
```

```python
import functools
import math

import jax
import jax.numpy as jnp
import numpy as np
from jax import lax
from jax.experimental import pallas as pl
from jax.experimental.pallas import tpu as pltpu

F32 = jnp.float32
BF16 = jnp.bfloat16

D_MODEL = 2048
D_ATT = 1024
ATT_HEADS = 8
HEAD_DIM = 128
ROT_DIM = 32
ROPE_THETA = 500000.0
N_SIDE = 64
M_HEADS = 4
M_QK = 128
M_V = 256
N_GATES = 16
NORM_EPS = 1e-6
NEG = -1e30

LANES = 128
MXU_COLS = 256
MLSTM_CHUNK = 256
PROJ_TM = 1024
PROJ_TN = 1024
PROJ_CHUNKS = (512, 512)
N_MAIN_STEPS = 7
OUT_TM = 512
VMEM_LIMIT = 56 * 1024 * 1024


LOG2_E = math.log2(math.e)


_NT = (((1,), (1,)), ((), ()))


def _proj_kernel(x_ref, nw_ref, w_ref, wqm_ref, wkm_ref, wg_ref, bias_ref, qw_ref, kw_ref,
                 cos_ref, sin_ref, rot_ref,
                 p_ref, kt_ref, gt_ref, xn_ref):
    n = pl.program_id(1)
    tm = x_ref.shape[0]

    @pl.when(n == 0)
    def _():
        x = x_ref[...]
        ms = jnp.mean(x * x, axis=-1, keepdims=True)
        xn_ref[...] = (x * lax.rsqrt(ms + NORM_EPS) * nw_ref[...]).astype(BF16)

    def column_chunks():
        lo = 0
        for width in PROJ_CHUNKS:
            acc = lax.dot_general(xn_ref[...], w_ref[lo:lo + width, :], _NT,
                                  preferred_element_type=F32)
            for i in range(width // MXU_COLS):
                yield (slice(lo + i * MXU_COLS, lo + (i + 1) * MXU_COLS),
                       acc[:, i * MXU_COLS:(i + 1) * MXU_COLS])
            lo += width

    @pl.when(n < 2)
    def _():
        heads = MXU_COLS // HEAD_DIM
        w_row = jnp.where(n == 0, qw_ref[...] * (HEAD_DIM ** -0.5 * LOG2_E), kw_ref[...])
        cos = jnp.concatenate([cos_ref[...]] * heads, axis=1)
        sin = jnp.concatenate([sin_ref[...]] * heads, axis=1)
        for sl, acc in column_chunks():
            sq = acc * acc
            ms = jnp.concatenate(
                [jnp.broadcast_to(jnp.mean(sq[:, h * HEAD_DIM:(h + 1) * HEAD_DIM], axis=1,
                                           keepdims=True), (tm, HEAD_DIM)) for h in range(heads)],
                axis=1)
            b = acc * w_row
            partner = jnp.dot(b.astype(BF16), rot_ref[...], preferred_element_type=F32)
            r = (b * cos + partner * sin) * lax.rsqrt(ms + NORM_EPS)
            p_ref[:, sl] = r.astype(BF16)

    @pl.when((n == 2) | (n == 4))
    def _():
        for sl, acc in column_chunks():
            p_ref[:, sl] = acc.astype(BF16)

    @pl.when((n == 3) | (n >= 5) & (n < N_MAIN_STEPS))
    def _():
        sig_only = n == 5
        for sl, acc in column_chunks():
            sg = 0.5 * jnp.tanh(0.5 * acc) + 0.5
            p_ref[:, sl] = (sg * jnp.where(sig_only, 1.0, acc)).astype(BF16)

    @pl.when(n == N_MAIN_STEPS)
    def _():
        xn = xn_ref[...]
        half = M_HEADS * M_QK
        qm = lax.dot_general(xn, wqm_ref[...], _NT, preferred_element_type=F32)
        p_ref[:, :half] = qm.astype(BF16)
        p_ref[:, half:] = jnp.zeros((tm, PROJ_TN - half), BF16)
        t = lax.dot_general(jnp.concatenate([wkm_ref[...], wg_ref[...]], axis=0), xn, _NT,
                            preferred_element_type=F32)
        for hh in range(M_HEADS):
            for j in range(tm // MLSTM_CHUNK):
                kt_ref[hh, j * M_QK:(j + 1) * M_QK, :] = (
                    t[hh * M_QK:(hh + 1) * M_QK, j * MLSTM_CHUNK:(j + 1) * MLSTM_CHUNK]
                    * (M_QK ** -0.5))
        g = t[half:, :]
        for j in range(tm // LANES):
            gt_ref[j * N_GATES:(j + 1) * N_GATES, :] = g[:, j * LANES:(j + 1) * LANES] + bias_ref[...]


def _main_weight_block(n):
    seg = jnp.minimum(n, N_MAIN_STEPS - 1)
    return seg + (seg >= 4).astype(jnp.int32)


def _input_projection(x2, norm_w, wt_bf, bias_bc, qw, kw, cos_t, sin_t, rot_m, seq):
    n_tok = x2.shape[0]
    tm, tn = PROJ_TM, PROJ_TN
    tiles_per_seq = seq // tm
    const = lambda m, n: (0, 0)
    grid = (n_tok // tm, N_MAIN_STEPS + 1)
    one = pl.Buffered(1)
    qm_rows = M_HEADS * M_QK
    qm_block = 4 * D_ATT // qm_rows
    gate_block = (wt_bf.shape[0] - N_GATES) // N_GATES
    return pl.pallas_call(
        _proj_kernel,
        grid=grid,
        in_specs=[
            pl.BlockSpec((tm, D_MODEL), lambda m, n: (m, 0)),
            pl.BlockSpec((1, D_MODEL), const, pipeline_mode=one),
            pl.BlockSpec((tn, D_MODEL), lambda m, n: (_main_weight_block(n), 0)),
            pl.BlockSpec((qm_rows, D_MODEL), lambda m, n: (qm_block, 0), pipeline_mode=one),
            pl.BlockSpec((qm_rows, D_MODEL), lambda m, n: (qm_block + 1, 0), pipeline_mode=one),
            pl.BlockSpec((N_GATES, D_MODEL), lambda m, n: (gate_block, 0), pipeline_mode=one),
            pl.BlockSpec(bias_bc.shape, const, pipeline_mode=one),
            pl.BlockSpec((1, MXU_COLS), const, pipeline_mode=one),
            pl.BlockSpec((1, MXU_COLS), const, pipeline_mode=one),
            pl.BlockSpec((tm, HEAD_DIM), lambda m, n: (m % tiles_per_seq, 0)),
            pl.BlockSpec((tm, HEAD_DIM), lambda m, n: (m % tiles_per_seq, 0)),
            pl.BlockSpec((MXU_COLS, MXU_COLS), const, pipeline_mode=one),
        ],
        out_specs=[
            pl.BlockSpec((tm, tn), lambda m, n: (m, n)),
            pl.BlockSpec((M_HEADS, tm // MLSTM_CHUNK * M_QK, MLSTM_CHUNK), lambda m, n: (0, m, 0)),
            pl.BlockSpec((tm // LANES * N_GATES, LANES), lambda m, n: (m, 0)),
        ],
        out_shape=[
            jax.ShapeDtypeStruct((n_tok, (N_MAIN_STEPS + 1) * tn), BF16),
            jax.ShapeDtypeStruct((M_HEADS, n_tok // MLSTM_CHUNK * M_QK, MLSTM_CHUNK), F32),
            jax.ShapeDtypeStruct((n_tok // LANES * N_GATES, LANES), F32),
        ],
        scratch_shapes=[pltpu.VMEM((tm, D_MODEL), BF16)],
        compiler_params=pltpu.CompilerParams(
            dimension_semantics=("arbitrary", "arbitrary"), vmem_limit_bytes=VMEM_LIMIT),
        name="in_proj",
    )(x2, norm_w, wt_bf, wt_bf, wt_bf, wt_bf, bias_bc, qw, kw, cos_t, sin_t, rot_m)


ATT_QT = 128
ATT_KT = ATT_QT + 2 * N_SIDE
ATT_GROUP = 32


def _band_bias(nq, nk, off):
    rel = (lax.broadcasted_iota(jnp.int32, (nq, nk), 1)
           - lax.broadcasted_iota(jnp.int32, (nq, nk), 0)) + off
    return jnp.where(jnp.abs(rel) <= N_SIDE, 0.0, NEG).astype(F32)


def _attn_kernel(q_ref, k_ref, v_ref, z_ref, o_ref, nat_f, res4_f, st4, st1, bias_s, bias16_s):
    seq = q_ref.shape[0]
    len4 = seq // 4
    for t, ref in enumerate((q_ref, k_ref, v_ref)):
        nat_f[t] = ref[...].astype(F32)
        for r in range(4):
            res4_f[t, r] = nat_f[t, pl.ds(r, len4, stride=4), :]
    nq16 = seq // 16

    @pl.when((pl.program_id(0) == 0) & (pl.program_id(1) == 0))
    def _():
        for j in range(3):
            bias_s[j] = _band_bias(ATT_QT, ATT_KT, -N_SIDE * j)
        bias16_s[...] = _band_bias(nq16, nq16, 0)

    ones = jnp.ones((ATT_KT, HEAD_DIM), BF16)

    def tile(q, k, v, bias, old):
        s = lax.dot_general(q, k, (((1,), (1,)), ((), ())), preferred_element_type=F32) + bias
        m_t = jnp.max(s, axis=1, keepdims=True)
        if old is None:
            m_n = jnp.broadcast_to(m_t, (q.shape[0], HEAD_DIM))
        else:
            m_o, l_o, a_o = old
            m_n = jnp.maximum(m_o, m_t)
        p = jnp.exp2(s - jnp.concatenate([m_n] * (s.shape[1] // HEAD_DIM), axis=1)).astype(BF16)
        r = jnp.dot(p, jnp.concatenate([v, ones], axis=1), preferred_element_type=F32)
        a_t, l_t = r[:, :HEAD_DIM], r[:, HEAD_DIM:]
        if old is None:
            return m_n, l_t, a_t
        alpha = jnp.exp2(m_o - m_n)
        return m_n, alpha * l_o + l_t, alpha * a_o + a_t

    def tile_geometry(i, length):
        kl = jnp.clip(ATT_QT * i - N_SIDE, 0, length - ATT_KT)
        return (pl.ds(pl.multiple_of(ATT_QT * i, ATT_QT), ATT_QT),
                pl.ds(pl.multiple_of(kl, N_SIDE), ATT_KT),
                (ATT_QT * i - kl) // N_SIDE)

    per_iter16 = ATT_GROUP // 8

    def body16(jg, carry):
        for jj in range(per_iter16):
            rows = pl.ds(jg * per_iter16 + jj, nq16, stride=4)
            for r in range(4):
                q, k, v = (res4_f[t, r, rows, :].astype(BF16) for t in range(3))
                state = tile(q, k, v, bias16_s[...], None)
                for t in range(3):
                    st4[t, r, rows, :] = state[t]
        return carry

    lax.fori_loop(0, 4 // per_iter16, body16, 0)

    def body4(ig, carry):
        for ii in range(ATT_GROUP // 4):
            i = ig * (ATT_GROUP // 4) + ii
            qs, ks, bi = tile_geometry(i, len4)
            bias = bias_s[bi]
            for r in range(4):
                state = tile(res4_f[0, r, qs, :].astype(BF16), res4_f[1, r, ks, :].astype(BF16),
                             res4_f[2, r, ks, :].astype(BF16), bias,
                             tuple(st4[t, r, qs, :] for t in range(3)))
                tok = pl.ds(r + 4 * ATT_QT * i, ATT_QT, stride=4)
                for t in range(3):
                    st1[t, tok, :] = state[t]
        return carry

    lax.fori_loop(0, len4 // ATT_QT // (ATT_GROUP // 4), body4, 0)

    def body1(ig, carry):
        for ii in range(ATT_GROUP):
            qs, ks, bi = tile_geometry(ig * ATT_GROUP + ii, seq)
            _, l, a = tile(q_ref[qs, :], k_ref[ks, :], v_ref[ks, :], bias_s[bi],
                           tuple(st1[t, qs, :] for t in range(3)))
            o_ref[qs, :] = (a / l * z_ref[qs, :].astype(F32)).astype(BF16)
        return carry

    lax.fori_loop(0, seq // ATT_QT // ATT_GROUP, body1, 0)


def _attention(p, batch, seq):
    blk = lambda col0: pl.BlockSpec((seq, HEAD_DIM), lambda b, h: (b, col0 + h))
    return pl.pallas_call(
        _attn_kernel,
        grid=(batch, ATT_HEADS),
        in_specs=[blk(0), blk(ATT_HEADS), blk(2 * ATT_HEADS), blk(3 * ATT_HEADS)],
        out_specs=pl.BlockSpec((seq, HEAD_DIM), lambda b, h: (b, h)),
        out_shape=jax.ShapeDtypeStruct((batch * seq, D_ATT), BF16),
        scratch_shapes=[
            pltpu.VMEM((3, seq, HEAD_DIM), F32),
            pltpu.VMEM((3, 4, seq // 4, HEAD_DIM), F32),
            pltpu.VMEM((3, 4, seq // 4, HEAD_DIM), F32),
            pltpu.VMEM((3, seq, HEAD_DIM), F32),
            pltpu.VMEM((3, ATT_QT, ATT_KT), F32),
            pltpu.VMEM((seq // 16, seq // 16), F32)],
        compiler_params=pltpu.CompilerParams(
            dimension_semantics=("arbitrary", "arbitrary"), vmem_limit_bytes=VMEM_LIMIT),
        name="dilated_attn",
    )(p, p, p, p)


def _scan_lanes(x, op, identity, reverse):
    width = x.shape[1]
    pos = lax.broadcasted_iota(jnp.int32, x.shape, 1)
    k = 1
    while k < width:
        if reverse:
            shifted = jnp.where(pos < width - k, pltpu.roll(x, width - k, 1), identity)
        else:
            shifted = jnp.where(pos >= k, pltpu.roll(x, k, 1), identity)
        x = op(x, shifted)
        k *= 2
    return x


def _mlstm_kernel(q_ref, kt_ref, v_ref, og_ref, zg_ref, g_ref, nw_ref, y_ref,
                  hbuf, c_ref, u_s, a_s, sig_s, mp_s, al_s, causal_s):
    head = pl.program_id(1)
    seq = q_ref.shape[0]
    tc = MLSTM_CHUNK
    nchunk = seq // tc
    blocks_per_chunk = tc // LANES

    def gate_rows(g):
        halves = [g_ref[pl.ds(hf * N_GATES + g, nchunk, stride=blocks_per_chunk * N_GATES), :]
                  for hf in range(blocks_per_chunk)]
        return jnp.concatenate(halves, axis=1)

    ones = jnp.ones((tc, LANES), BF16)

    def prepare(d, h):
        reverse = d == 1
        ch = d * M_HEADS + h
        li = gate_rows(2 * d * M_HEADS + h) * LOG2_E
        fp = gate_rows((2 * d + 1) * M_HEADS + h)
        lf = (jnp.minimum(fp, 0.0) - jnp.log1p(jnp.exp(-jnp.abs(fp)))) * LOG2_E
        b = _scan_lanes(lf, jnp.add, 0.0, reverse)
        u = li - b
        um = _scan_lanes(u, jnp.maximum, NEG, reverse)
        u_s[ch] = u
        m_prev = jnp.full((1, tc), NEG, F32)
        order = range(nchunk - 1, -1, -1) if reverse else range(nchunk)
        for c in order:
            u_max = jnp.max(u[c:c + 1, :], axis=1, keepdims=True)
            b_end = jnp.sum(lf[c:c + 1, :], axis=1, keepdims=True)
            a_end = jnp.maximum(m_prev, u_max)
            mp_s[ch, c:c + 1, :] = m_prev
            al_s[ch, c:c + 1, :] = a_end
            m_prev = b_end + a_end
        a = jnp.maximum(mp_s[ch], um)
        a_s[ch] = a
        sig_s[ch] = b + a

    @pl.when(head == 0)
    def _():
        for d in range(2):
            for h in range(M_HEADS):
                prepare(d, h)
        row_i = lax.broadcasted_iota(jnp.int32, (tc, tc), 0)
        col_i = lax.broadcasted_iota(jnp.int32, (tc, tc), 1)
        causal_s[0] = jnp.where(col_i <= row_i, 0.0, NEG).astype(F32)
        causal_s[1] = jnp.where(col_i >= row_i, 0.0, NEG).astype(F32)

    def to_col(row):
        return jnp.transpose(jnp.broadcast_to(row, (LANES, tc)))

    def chunk(d, c):
        reverse = d == 1
        tok = pl.ds(pl.multiple_of(c * tc, tc), tc)
        q = q_ref[tok, :]
        kt = kt_ref[pl.ds(pl.multiple_of(c * M_QK, M_QK), M_QK), :]
        v = v_ref[tok, :]
        va = jnp.concatenate([v, ones], axis=1)
        ch = d * M_HEADS + head
        u = u_s[ch, pl.ds(c, 1), :]
        a_col = to_col(a_s[ch, pl.ds(c, 1), :])
        sig_col = to_col(sig_s[ch, pl.ds(c, 1), :])
        m_prev = mp_s[ch, pl.ds(c, 1), :]
        a_end = al_s[ch, pl.ds(c, 1), :]

        s = jnp.dot(q, kt.astype(BF16), preferred_element_type=F32)
        arg = u - jnp.concatenate([a_col] * (tc // LANES), axis=1)
        sw = s * jnp.exp2(arg + causal_s[d])
        intra = jnp.dot(sw.astype(BF16), v, preferred_element_type=F32)
        c_aug = c_ref[d]
        inter = jnp.dot(q, c_aug.astype(BF16), preferred_element_type=F32)
        w_inter = jnp.exp2(m_prev[:, :LANES] - a_col)
        num = jnp.concatenate([w_inter] * (M_V // LANES), axis=1) * inter[:, :M_V] + intra
        den = w_inter * inter[:, M_V:] + jnp.sum(sw, axis=1, keepdims=True)
        inv = 1.0 / jnp.maximum(jnp.abs(den), jnp.exp2(-sig_col))
        h = num * jnp.concatenate([inv] * (M_V // LANES), axis=1)
        wk = jnp.exp2(u - a_end)
        upd = jnp.dot((kt * wk).astype(BF16), va, preferred_element_type=F32)
        decay = jnp.exp2(m_prev[:, :LANES] - a_end[:, :LANES])
        c_ref[d] = jnp.concatenate([decay] * (c_aug.shape[1] // LANES), axis=1) * c_aug + upd
        return h

    c_ref[...] = jnp.zeros(c_ref.shape, F32)

    def finalize(c, h):
        tok = pl.ds(pl.multiple_of(c * tc, tc), tc)
        hs = hbuf[tok, :] + h
        ms = jnp.mean(hs * hs, axis=-1, keepdims=True)
        y = hs * lax.rsqrt(ms + NORM_EPS) * nw_ref[...]
        y = og_ref[tok, :].astype(F32) * y * zg_ref[tok, :].astype(F32)
        y_ref[tok, :] = y.astype(BF16)

    def first_half(i, carry):
        cb = nchunk - 1 - i
        hbuf[pl.ds(pl.multiple_of(i * tc, tc), tc), :] = chunk(0, i)
        hbuf[pl.ds(pl.multiple_of(cb * tc, tc), tc), :] = chunk(1, cb)
        return carry

    def second_half(i, carry):
        cb = nchunk - 1 - i
        finalize(i, chunk(0, i))
        finalize(cb, chunk(1, cb))
        return carry

    lax.fori_loop(0, nchunk // 2, first_half, 0, unroll=4)
    lax.fori_loop(nchunk // 2, nchunk, second_half, 0, unroll=4)


def _mlstm(p, kt, gt, m_norm_w, batch, seq):
    nchunk = seq // MLSTM_CHUNK
    qm_col0 = N_MAIN_STEPS * PROJ_TN // M_QK
    vblk = lambda seg: pl.BlockSpec((seq, M_V), lambda b, h: (b, seg * (PROJ_TN // M_V) + h))
    rows = pltpu.VMEM((2 * M_HEADS, nchunk, MLSTM_CHUNK), F32)
    return pl.pallas_call(
        _mlstm_kernel,
        grid=(batch, M_HEADS),
        in_specs=[
            pl.BlockSpec((seq, M_QK), lambda b, h: (b, qm_col0 + h)),
            pl.BlockSpec((None, nchunk * M_QK, MLSTM_CHUNK), lambda b, h: (h, b, 0)),
            vblk(4), vblk(5), vblk(6),
            pl.BlockSpec((seq // LANES * N_GATES, LANES), lambda b, h: (b, 0)),
            pl.BlockSpec((1, M_V), lambda b, h: (0, h)),
        ],
        out_specs=pl.BlockSpec((seq, M_V), lambda b, h: (b, h)),
        out_shape=jax.ShapeDtypeStruct((batch * seq, M_HEADS * M_V), BF16),
        scratch_shapes=[
            pltpu.VMEM((seq, M_V), F32),
            pltpu.VMEM((2, M_QK, M_V + LANES), F32),
            rows, rows, rows, rows, rows,
            pltpu.VMEM((2, MLSTM_CHUNK, MLSTM_CHUNK), F32),
        ],
        compiler_params=pltpu.CompilerParams(
            dimension_semantics=("arbitrary", "arbitrary"), vmem_limit_bytes=VMEM_LIMIT),
        name="mlstm",
    )(p, kt, p, p, p, gt, m_norm_w)


def _out_kernel(x_ref, ya_ref, ym_ref, w_ref, o_ref, wb_ref):
    @pl.when(pl.program_id(0) == 0)
    def _():
        wb_ref[...] = w_ref[...].astype(BF16)

    d_a = ya_ref.shape[1]
    acc = jnp.dot(ya_ref[...], wb_ref[:d_a, :], preferred_element_type=F32)
    acc = acc + jnp.dot(ym_ref[...], wb_ref[d_a:, :], preferred_element_type=F32)
    o_ref[...] = x_ref[...] + acc


def _output_projection(x2, ya, ym, w_out):
    n_tok = x2.shape[0]
    tm = OUT_TM
    return pl.pallas_call(
        _out_kernel,
        grid=(n_tok // tm,),
        in_specs=[
            pl.BlockSpec((tm, D_MODEL), lambda m: (m, 0)),
            pl.BlockSpec((tm, ya.shape[1]), lambda m: (m, 0)),
            pl.BlockSpec((tm, ym.shape[1]), lambda m: (m, 0)),
            pl.BlockSpec(w_out.shape, lambda m: (0, 0), pipeline_mode=pl.Buffered(1)),
        ],
        out_specs=pl.BlockSpec((tm, D_MODEL), lambda m: (m, 0)),
        out_shape=jax.ShapeDtypeStruct((n_tok, D_MODEL), F32),
        scratch_shapes=[pltpu.VMEM(w_out.shape, BF16)],
        compiler_params=pltpu.CompilerParams(
            dimension_semantics=("arbitrary",), vmem_limit_bytes=VMEM_LIMIT),
        name="out_proj",
    )(x2, ya, ym, w_out)


def _rope_tables(seq):
    pos = np.arange(seq, dtype=np.float64)
    inv_freq = ROPE_THETA ** (-np.arange(0, ROT_DIM, 2, dtype=np.float64) / ROT_DIM)
    lane = np.arange(HEAD_DIM)
    freq = np.where(lane < ROT_DIM, inv_freq[lane % (ROT_DIM // 2)], 0.0)
    ang = pos[:, None] * freq[None, :]
    return jnp.asarray(np.cos(ang), F32), jnp.asarray(np.sin(ang), F32)


def _rotate_half_matrix():
    half = ROT_DIM // 2
    src = np.arange(MXU_COLS)[:, None]
    dst = np.arange(MXU_COLS)[None, :]
    same_head = (src // HEAD_DIM) == (dst // HEAD_DIM)
    d = dst % HEAD_DIM
    rot = (np.where(same_head & (d < half) & (src == dst + half), -1.0, 0.0)
           + np.where(same_head & (d >= half) & (d < ROT_DIM) & (src == dst - half), 1.0, 0.0))
    return (jnp.asarray(rot, BF16),)


def _layer(x, norm_w, w_in, b_gates, q_norm_w, k_norm_w, m_norm_w, w_out, tables):
    batch, seq, _ = x.shape
    x2 = x.reshape(batch * seq, D_MODEL)
    wt_bf = w_in.T.astype(BF16)
    bias_bc = jnp.broadcast_to(b_gates.astype(F32).reshape(N_GATES, 1), (N_GATES, LANES))
    heads = MXU_COLS // HEAD_DIM
    p, kt, gt = _input_projection(
        x2, norm_w.reshape(1, D_MODEL), wt_bf, bias_bc,
        jnp.tile(q_norm_w.reshape(1, HEAD_DIM), (1, heads)),
        jnp.tile(k_norm_w.reshape(1, HEAD_DIM), (1, heads)), *tables, seq)
    ya = _attention(p, batch, seq)
    ym = _mlstm(p, kt, gt, m_norm_w.reshape(1, M_HEADS * M_V), batch, seq)
    out = _output_projection(x2, ya, ym, w_out)
    return out.reshape(x.shape)


def kernel(x, norm_w, w_in, b_gates, q_norm_w, k_norm_w, m_norm_w, w_out):
    tables = _rope_tables(x.shape[1]) + _rotate_half_matrix()
    for layer in range(norm_w.shape[0]):
        x = _layer(x, norm_w[layer], w_in[layer], b_gates[layer], q_norm_w[layer],
                   k_norm_w[layer], m_norm_w[layer], w_out[layer], tables)
    return x
```

```python
import functools
import math

import jax
import jax.numpy as jnp
import numpy as np
from jax import lax
from jax.experimental import pallas as pl
from jax.experimental.pallas import tpu as pltpu

F32 = jnp.float32
BF16 = jnp.bfloat16

D_MODEL = 2048
D_ATT = 1024
ATT_HEADS = 8
HEAD_DIM = 128
ROT_DIM = 32
ROPE_THETA = 500000.0
N_SIDE = 64
M_HEADS = 4
M_QK = 128
M_V = 256
N_GATES = 16
NORM_EPS = 1e-6
NEG = -1e30

LANES = 128
MXU_COLS = 256
MLSTM_CHUNK = 256
PROJ_TM = 1024
PROJ_TN = 1024
PROJ_CHUNKS = (512, 512)
N_MAIN_STEPS = 7
OUT_TM = 512
VMEM_LIMIT = 56 * 1024 * 1024


LOG2_E = math.log2(math.e)


_NT = (((1,), (1,)), ((), ()))


def _proj_kernel(x_ref, nw_ref, w_ref, wqm_ref, wkm_ref, wg_ref, bias_ref, qw_ref, kw_ref,
                 cos_ref, sin_ref, rot_ref,
                 p_ref, kt_ref, gt_ref, xn_ref):
    n = pl.program_id(1)
    tm = x_ref.shape[0]

    @pl.when(n == 0)
    def _():
        x = x_ref[...]
        ms = jnp.mean(x * x, axis=-1, keepdims=True)
        xn_ref[...] = (x * lax.rsqrt(ms + NORM_EPS) * nw_ref[...]).astype(BF16)

    def column_chunks():
        lo = 0
        for width in PROJ_CHUNKS:
            acc = lax.dot_general(xn_ref[...], w_ref[lo:lo + width, :], _NT,
                                  preferred_element_type=F32)
            for i in range(width // MXU_COLS):
                yield (slice(lo + i * MXU_COLS, lo + (i + 1) * MXU_COLS),
                       acc[:, i * MXU_COLS:(i + 1) * MXU_COLS])
            lo += width

    @pl.when(n < 2)
    def _():
        heads = MXU_COLS // HEAD_DIM
        w_row = jnp.where(n == 0, qw_ref[...] * (HEAD_DIM ** -0.5 * LOG2_E), kw_ref[...])
        cos = jnp.concatenate([cos_ref[...]] * heads, axis=1)
        sin = jnp.concatenate([sin_ref[...]] * heads, axis=1)
        for sl, acc in column_chunks():
            sq = acc * acc
            ms = jnp.concatenate(
                [jnp.broadcast_to(jnp.mean(sq[:, h * HEAD_DIM:(h + 1) * HEAD_DIM], axis=1,
                                           keepdims=True), (tm, HEAD_DIM)) for h in range(heads)],
                axis=1)
            b = acc * w_row
            partner = jnp.dot(b.astype(BF16), rot_ref[...], preferred_element_type=F32)
            r = (b * cos + partner * sin) * lax.rsqrt(ms + NORM_EPS)
            p_ref[:, sl] = r.astype(BF16)

    @pl.when((n == 2) | (n == 4))
    def _():
        for sl, acc in column_chunks():
            p_ref[:, sl] = acc.astype(BF16)

    @pl.when((n == 3) | (n >= 5) & (n < N_MAIN_STEPS))
    def _():
        sig_only = n == 5
        for sl, acc in column_chunks():
            sg = 0.5 * jnp.tanh(0.5 * acc) + 0.5
            p_ref[:, sl] = (sg * jnp.where(sig_only, 1.0, acc)).astype(BF16)

    @pl.when(n == N_MAIN_STEPS)
    def _():
        xn = xn_ref[...]
        half = M_HEADS * M_QK
        qm = lax.dot_general(xn, wqm_ref[...], _NT, preferred_element_type=F32)
        p_ref[:, :half] = qm.astype(BF16)
        p_ref[:, half:] = jnp.zeros((tm, PROJ_TN - half), BF16)
        t = lax.dot_general(jnp.concatenate([wkm_ref[...], wg_ref[...]], axis=0), xn, _NT,
                            preferred_element_type=F32)
        for hh in range(M_HEADS):
            for j in range(tm // MLSTM_CHUNK):
                kt_ref[hh, j * M_QK:(j + 1) * M_QK, :] = (
                    t[hh * M_QK:(hh + 1) * M_QK, j * MLSTM_CHUNK:(j + 1) * MLSTM_CHUNK]
                    * (M_QK ** -0.5))
        g = t[half:, :]
        for j in range(tm // LANES):
            gt_ref[j * N_GATES:(j + 1) * N_GATES, :] = g[:, j * LANES:(j + 1) * LANES] + bias_ref[...]


def _main_weight_block(n):
    seg = jnp.minimum(n, N_MAIN_STEPS - 1)
    return seg + (seg >= 4).astype(jnp.int32)


def _input_projection(x2, norm_w, wt_bf, bias_bc, qw, kw, cos_t, sin_t, rot_m, seq):
    n_tok = x2.shape[0]
    tm, tn = PROJ_TM, PROJ_TN
    tiles_per_seq = seq // tm
    const = lambda m, n: (0, 0)
    grid = (n_tok // tm, N_MAIN_STEPS + 1)
    one = pl.Buffered(1)
    qm_rows = M_HEADS * M_QK
    qm_block = 4 * D_ATT // qm_rows
    gate_block = (wt_bf.shape[0] - N_GATES) // N_GATES
    return pl.pallas_call(
        _proj_kernel,
        grid=grid,
        in_specs=[
            pl.BlockSpec((tm, D_MODEL), lambda m, n: (m, 0)),
            pl.BlockSpec((1, D_MODEL), const, pipeline_mode=one),
            pl.BlockSpec((tn, D_MODEL), lambda m, n: (_main_weight_block(n), 0)),
            pl.BlockSpec((qm_rows, D_MODEL), lambda m, n: (qm_block, 0), pipeline_mode=one),
            pl.BlockSpec((qm_rows, D_MODEL), lambda m, n: (qm_block + 1, 0), pipeline_mode=one),
            pl.BlockSpec((N_GATES, D_MODEL), lambda m, n: (gate_block, 0), pipeline_mode=one),
            pl.BlockSpec(bias_bc.shape, const, pipeline_mode=one),
            pl.BlockSpec((1, MXU_COLS), const, pipeline_mode=one),
            pl.BlockSpec((1, MXU_COLS), const, pipeline_mode=one),
            pl.BlockSpec((tm, HEAD_DIM), lambda m, n: (m % tiles_per_seq, 0)),
            pl.BlockSpec((tm, HEAD_DIM), lambda m, n: (m % tiles_per_seq, 0)),
            pl.BlockSpec((MXU_COLS, MXU_COLS), const, pipeline_mode=one),
        ],
        out_specs=[
            pl.BlockSpec((tm, tn), lambda m, n: (m, n)),
            pl.BlockSpec((M_HEADS, tm // MLSTM_CHUNK * M_QK, MLSTM_CHUNK), lambda m, n: (0, m, 0)),
            pl.BlockSpec((tm // LANES * N_GATES, LANES), lambda m, n: (m, 0)),
        ],
        out_shape=[
            jax.ShapeDtypeStruct((n_tok, (N_MAIN_STEPS + 1) * tn), BF16),
            jax.ShapeDtypeStruct((M_HEADS, n_tok // MLSTM_CHUNK * M_QK, MLSTM_CHUNK), F32),
            jax.ShapeDtypeStruct((n_tok // LANES * N_GATES, LANES), F32),
        ],
        scratch_shapes=[pltpu.VMEM((tm, D_MODEL), BF16)],
        compiler_params=pltpu.CompilerParams(
            dimension_semantics=("arbitrary", "arbitrary"), vmem_limit_bytes=VMEM_LIMIT),
        name="in_proj",
    )(x2, norm_w, wt_bf, wt_bf, wt_bf, wt_bf, bias_bc, qw, kw, cos_t, sin_t, rot_m)


ATT_QT = 128
ATT_KT = ATT_QT + 2 * N_SIDE
ATT_GROUP = 32


def _band_bias(nq, nk, off):
    rel = (lax.broadcasted_iota(jnp.int32, (nq, nk), 1)
           - lax.broadcasted_iota(jnp.int32, (nq, nk), 0)) + off
    return jnp.where(jnp.abs(rel) <= N_SIDE, 0.0, NEG).astype(F32)


def _attn_kernel(q_ref, k_ref, v_ref, z_ref, o_ref, nat_f, res4_f, st4, st1, bias_s, bias16_s):
    seq = q_ref.shape[0]
    len4 = seq // 4
    for t, ref in enumerate((q_ref, k_ref, v_ref)):
        nat_f[t] = ref[...].astype(F32)
        for r in range(4):
            res4_f[t, r] = nat_f[t, pl.ds(r, len4, stride=4), :]
    nq16 = seq // 16

    @pl.when((pl.program_id(0) == 0) & (pl.program_id(1) == 0))
    def _():
        for j in range(3):
            bias_s[j] = _band_bias(ATT_QT, ATT_KT, -N_SIDE * j)
        bias16_s[...] = _band_bias(nq16, nq16, 0)

    ones = jnp.ones((ATT_KT, HEAD_DIM), BF16)

    def tile(q, k, v, bias, old):
        s = lax.dot_general(q, k, (((1,), (1,)), ((), ())), preferred_element_type=F32) + bias
        m_t = jnp.max(s, axis=1, keepdims=True)
        if old is None:
            m_n = jnp.broadcast_to(m_t, (q.shape[0], HEAD_DIM))
        else:
            m_o, l_o, a_o = old
            m_n = jnp.maximum(m_o, m_t)
        p = jnp.exp2(s - jnp.concatenate([m_n] * (s.shape[1] // HEAD_DIM), axis=1)).astype(BF16)
        r = jnp.dot(p, jnp.concatenate([v, ones], axis=1), preferred_element_type=F32)
        a_t, l_t = r[:, :HEAD_DIM], r[:, HEAD_DIM:]
        if old is None:
            return m_n, l_t, a_t
        alpha = jnp.exp2(m_o - m_n)
        return m_n, alpha * l_o + l_t, alpha * a_o + a_t

    def tile_geometry(i, length):
        kl = jnp.clip(ATT_QT * i - N_SIDE, 0, length - ATT_KT)
        return (pl.ds(pl.multiple_of(ATT_QT * i, ATT_QT), ATT_QT),
                pl.ds(pl.multiple_of(kl, N_SIDE), ATT_KT),
                (ATT_QT * i - kl) // N_SIDE)

    per_iter16 = ATT_GROUP // 8

    def body16(jg, carry):
        for jj in range(per_iter16):
            rows = pl.ds(jg * per_iter16 + jj, nq16, stride=4)
            for r in range(4):
                q, k, v = (res4_f[t, r, rows, :].astype(BF16) for t in range(3))
                state = tile(q, k, v, bias16_s[...], None)
                for t in range(3):
                    st4[t, r, rows, :] = state[t]
        return carry

    lax.fori_loop(0, 4 // per_iter16, body16, 0)

    def body4(ig, carry):
        for ii in range(ATT_GROUP // 4):
            i = ig * (ATT_GROUP // 4) + ii
            qs, ks, bi = tile_geometry(i, len4)
            bias = bias_s[bi]
            for r in range(4):
                state = tile(res4_f[0, r, qs, :].astype(BF16), res4_f[1, r, ks, :].astype(BF16),
                             res4_f[2, r, ks, :].astype(BF16), bias,
                             tuple(st4[t, r, qs, :] for t in range(3)))
                tok = pl.ds(r + 4 * ATT_QT * i, ATT_QT, stride=4)
                for t in range(3):
                    st1[t, tok, :] = state[t]
        return carry

    lax.fori_loop(0, len4 // ATT_QT // (ATT_GROUP // 4), body4, 0)

    def body1(ig, carry):
        for ii in range(ATT_GROUP):
            qs, ks, bi = tile_geometry(ig * ATT_GROUP + ii, seq)
            _, l, a = tile(q_ref[qs, :], k_ref[ks, :], v_ref[ks, :], bias_s[bi],
                           tuple(st1[t, qs, :] for t in range(3)))
            o_ref[qs, :] = (a / l * z_ref[qs, :].astype(F32)).astype(BF16)
        return carry

    lax.fori_loop(0, seq // ATT_QT // ATT_GROUP, body1, 0)


def _attention(p, batch, seq):
    blk = lambda col0: pl.BlockSpec((seq, HEAD_DIM), lambda b, h: (b, col0 + h))
    return pl.pallas_call(
        _attn_kernel,
        grid=(batch, ATT_HEADS),
        in_specs=[blk(0), blk(ATT_HEADS), blk(2 * ATT_HEADS), blk(3 * ATT_HEADS)],
        out_specs=pl.BlockSpec((seq, HEAD_DIM), lambda b, h: (b, h)),
        out_shape=jax.ShapeDtypeStruct((batch * seq, D_ATT), BF16),
        scratch_shapes=[
            pltpu.VMEM((3, seq, HEAD_DIM), F32),
            pltpu.VMEM((3, 4, seq // 4, HEAD_DIM), F32),
            pltpu.VMEM((3, 4, seq // 4, HEAD_DIM), F32),
            pltpu.VMEM((3, seq, HEAD_DIM), F32),
            pltpu.VMEM((3, ATT_QT, ATT_KT), F32),
            pltpu.VMEM((seq // 16, seq // 16), F32)],
        compiler_params=pltpu.CompilerParams(
            dimension_semantics=("arbitrary", "arbitrary"), vmem_limit_bytes=VMEM_LIMIT),
        name="dilated_attn",
    )(p, p, p, p)


def _scan_lanes(x, op, identity, reverse):
    width = x.shape[1]
    pos = lax.broadcasted_iota(jnp.int32, x.shape, 1)
    k = 1
    while k < width:
        if reverse:
            shifted = jnp.where(pos < width - k, pltpu.roll(x, width - k, 1), identity)
        else:
            shifted = jnp.where(pos >= k, pltpu.roll(x, k, 1), identity)
        x = op(x, shifted)
        k *= 2
    return x


def _mlstm_kernel(q_ref, kt_ref, v_ref, og_ref, zg_ref, g_ref, nw_ref, y_ref,
                  hbuf, c_ref, u_s, a_s, sig_s, mp_s, al_s, causal_s):
    head = pl.program_id(1)
    seq = q_ref.shape[0]
    tc = MLSTM_CHUNK
    nchunk = seq // tc
    blocks_per_chunk = tc // LANES

    def gate_rows(g):
        halves = [g_ref[pl.ds(hf * N_GATES + g, nchunk, stride=blocks_per_chunk * N_GATES), :]
                  for hf in range(blocks_per_chunk)]
        return jnp.concatenate(halves, axis=1)

    ones = jnp.ones((tc, LANES), BF16)

    def prepare(d, h):
        reverse = d == 1
        ch = d * M_HEADS + h
        li = gate_rows(2 * d * M_HEADS + h) * LOG2_E
        fp = gate_rows((2 * d + 1) * M_HEADS + h)
        lf = (jnp.minimum(fp, 0.0) - jnp.log1p(jnp.exp(-jnp.abs(fp)))) * LOG2_E
        b = _scan_lanes(lf, jnp.add, 0.0, reverse)
        u = li - b
        um = _scan_lanes(u, jnp.maximum, NEG, reverse)
        u_s[ch] = u
        m_prev = jnp.full((1, tc), NEG, F32)
        order = range(nchunk - 1, -1, -1) if reverse else range(nchunk)
        for c in order:
            u_max = jnp.max(u[c:c + 1, :], axis=1, keepdims=True)
            b_end = jnp.sum(lf[c:c + 1, :], axis=1, keepdims=True)
            a_end = jnp.maximum(m_prev, u_max)
            mp_s[ch, c:c + 1, :] = m_prev
            al_s[ch, c:c + 1, :] = a_end
            m_prev = b_end + a_end
        a = jnp.maximum(mp_s[ch], um)
        a_s[ch] = a
        sig_s[ch] = b + a

    @pl.when(head == 0)
    def _():
        for d in range(2):
            for h in range(M_HEADS):
                prepare(d, h)
        row_i = lax.broadcasted_iota(jnp.int32, (tc, tc), 0)
        col_i = lax.broadcasted_iota(jnp.int32, (tc, tc), 1)
        causal_s[0] = jnp.where(col_i <= row_i, 0.0, NEG).astype(F32)
        causal_s[1] = jnp.where(col_i >= row_i, 0.0, NEG).astype(F32)

    def to_col(row):
        return jnp.transpose(jnp.broadcast_to(row, (LANES, tc)))

    def chunk(d, c):
        reverse = d == 1
        tok = pl.ds(pl.multiple_of(c * tc, tc), tc)
        q = q_ref[tok, :]
        kt = kt_ref[pl.ds(pl.multiple_of(c * M_QK, M_QK), M_QK), :]
        v = v_ref[tok, :]
        va = jnp.concatenate([v, ones], axis=1)
        ch = d * M_HEADS + head
        u = u_s[ch, pl.ds(c, 1), :]
        a_col = to_col(a_s[ch, pl.ds(c, 1), :])
        sig_col = to_col(sig_s[ch, pl.ds(c, 1), :])
        m_prev = mp_s[ch, pl.ds(c, 1), :]
        a_end = al_s[ch, pl.ds(c, 1), :]

        c_aug = c_ref[d]
        qkc = jnp.dot(q, jnp.concatenate([kt.astype(BF16), c_aug.astype(BF16)], axis=1),
                      preferred_element_type=F32)
        s, inter = qkc[:, :tc], qkc[:, tc:]
        arg = u - jnp.concatenate([a_col] * (tc // LANES), axis=1)
        sw = s * jnp.exp2(arg + causal_s[d])
        intra = jnp.dot(sw.astype(BF16), v, preferred_element_type=F32)
        w_inter = jnp.exp2(m_prev[:, :LANES] - a_col)
        num = jnp.concatenate([w_inter] * (M_V // LANES), axis=1) * inter[:, :M_V] + intra
        den = w_inter * inter[:, M_V:] + jnp.sum(sw, axis=1, keepdims=True)
        inv = 1.0 / jnp.maximum(jnp.abs(den), jnp.exp2(-sig_col))
        h = num * jnp.concatenate([inv] * (M_V // LANES), axis=1)
        wk = jnp.exp2(u - a_end)
        upd = jnp.dot((kt * wk).astype(BF16), va, preferred_element_type=F32)
        decay = jnp.exp2(m_prev[:, :LANES] - a_end[:, :LANES])
        c_ref[d] = jnp.concatenate([decay] * (c_aug.shape[1] // LANES), axis=1) * c_aug + upd
        return h

    c_ref[...] = jnp.zeros(c_ref.shape, F32)

    def finalize(c, h):
        tok = pl.ds(pl.multiple_of(c * tc, tc), tc)
        hs = hbuf[tok, :] + h
        ms = jnp.mean(hs * hs, axis=-1, keepdims=True)
        y = hs * lax.rsqrt(ms + NORM_EPS) * nw_ref[...]
        y = og_ref[tok, :].astype(F32) * y * zg_ref[tok, :].astype(F32)
        y_ref[tok, :] = y.astype(BF16)

    def first_half(i, carry):
        cb = nchunk - 1 - i
        hbuf[pl.ds(pl.multiple_of(i * tc, tc), tc), :] = chunk(0, i)
        hbuf[pl.ds(pl.multiple_of(cb * tc, tc), tc), :] = chunk(1, cb)
        return carry

    def second_half(i, carry):
        cb = nchunk - 1 - i
        finalize(i, chunk(0, i))
        finalize(cb, chunk(1, cb))
        return carry

    lax.fori_loop(0, nchunk // 2, first_half, 0, unroll=4)
    lax.fori_loop(nchunk // 2, nchunk, second_half, 0, unroll=4)


def _mlstm(p, kt, gt, m_norm_w, batch, seq):
    nchunk = seq // MLSTM_CHUNK
    qm_col0 = N_MAIN_STEPS * PROJ_TN // M_QK
    vblk = lambda seg: pl.BlockSpec((seq, M_V), lambda b, h: (b, seg * (PROJ_TN // M_V) + h))
    rows = pltpu.VMEM((2 * M_HEADS, nchunk, MLSTM_CHUNK), F32)
    return pl.pallas_call(
        _mlstm_kernel,
        grid=(batch, M_HEADS),
        in_specs=[
            pl.BlockSpec((seq, M_QK), lambda b, h: (b, qm_col0 + h)),
            pl.BlockSpec((None, nchunk * M_QK, MLSTM_CHUNK), lambda b, h: (h, b, 0)),
            vblk(4), vblk(5), vblk(6),
            pl.BlockSpec((seq // LANES * N_GATES, LANES), lambda b, h: (b, 0)),
            pl.BlockSpec((1, M_V), lambda b, h: (0, h)),
        ],
        out_specs=pl.BlockSpec((seq, M_V), lambda b, h: (b, h)),
        out_shape=jax.ShapeDtypeStruct((batch * seq, M_HEADS * M_V), BF16),
        scratch_shapes=[
            pltpu.VMEM((seq, M_V), F32),
            pltpu.VMEM((2, M_QK, M_V + LANES), F32),
            rows, rows, rows, rows, rows,
            pltpu.VMEM((2, MLSTM_CHUNK, MLSTM_CHUNK), F32),
        ],
        compiler_params=pltpu.CompilerParams(
            dimension_semantics=("arbitrary", "arbitrary"), vmem_limit_bytes=VMEM_LIMIT),
        name="mlstm",
    )(p, kt, p, p, p, gt, m_norm_w)


def _out_kernel(x_ref, ya_ref, ym_ref, w_ref, o_ref, wb_ref):
    @pl.when(pl.program_id(0) == 0)
    def _():
        wb_ref[...] = w_ref[...].astype(BF16)

    d_a = ya_ref.shape[1]
    acc = jnp.dot(ya_ref[...], wb_ref[:d_a, :], preferred_element_type=F32)
    acc = acc + jnp.dot(ym_ref[...], wb_ref[d_a:, :], preferred_element_type=F32)
    o_ref[...] = x_ref[...] + acc


def _output_projection(x2, ya, ym, w_out):
    n_tok = x2.shape[0]
    tm = OUT_TM
    return pl.pallas_call(
        _out_kernel,
        grid=(n_tok // tm,),
        in_specs=[
            pl.BlockSpec((tm, D_MODEL), lambda m: (m, 0)),
            pl.BlockSpec((tm, ya.shape[1]), lambda m: (m, 0)),
            pl.BlockSpec((tm, ym.shape[1]), lambda m: (m, 0)),
            pl.BlockSpec(w_out.shape, lambda m: (0, 0), pipeline_mode=pl.Buffered(1)),
        ],
        out_specs=pl.BlockSpec((tm, D_MODEL), lambda m: (m, 0)),
        out_shape=jax.ShapeDtypeStruct((n_tok, D_MODEL), F32),
        scratch_shapes=[pltpu.VMEM(w_out.shape, BF16)],
        compiler_params=pltpu.CompilerParams(
            dimension_semantics=("arbitrary",), vmem_limit_bytes=VMEM_LIMIT),
        name="out_proj",
    )(x2, ya, ym, w_out)


def _rope_tables(seq):
    pos = np.arange(seq, dtype=np.float64)
    inv_freq = ROPE_THETA ** (-np.arange(0, ROT_DIM, 2, dtype=np.float64) / ROT_DIM)
    lane = np.arange(HEAD_DIM)
    freq = np.where(lane < ROT_DIM, inv_freq[lane % (ROT_DIM // 2)], 0.0)
    ang = pos[:, None] * freq[None, :]
    return jnp.asarray(np.cos(ang), F32), jnp.asarray(np.sin(ang), F32)


def _rotate_half_matrix():
    half = ROT_DIM // 2
    src = np.arange(MXU_COLS)[:, None]
    dst = np.arange(MXU_COLS)[None, :]
    same_head = (src // HEAD_DIM) == (dst // HEAD_DIM)
    d = dst % HEAD_DIM
    rot = (np.where(same_head & (d < half) & (src == dst + half), -1.0, 0.0)
           + np.where(same_head & (d >= half) & (d < ROT_DIM) & (src == dst - half), 1.0, 0.0))
    return (jnp.asarray(rot, BF16),)


def _layer(x, norm_w, w_in, b_gates, q_norm_w, k_norm_w, m_norm_w, w_out, tables):
    batch, seq, _ = x.shape
    x2 = x.reshape(batch * seq, D_MODEL)
    wt_bf = w_in.T.astype(BF16)
    bias_bc = jnp.broadcast_to(b_gates.astype(F32).reshape(N_GATES, 1), (N_GATES, LANES))
    heads = MXU_COLS // HEAD_DIM
    p, kt, gt = _input_projection(
        x2, norm_w.reshape(1, D_MODEL), wt_bf, bias_bc,
        jnp.tile(q_norm_w.reshape(1, HEAD_DIM), (1, heads)),
        jnp.tile(k_norm_w.reshape(1, HEAD_DIM), (1, heads)), *tables, seq)
    ya = _attention(p, batch, seq)
    ym = _mlstm(p, kt, gt, m_norm_w.reshape(1, M_HEADS * M_V), batch, seq)
    out = _output_projection(x2, ya, ym, w_out)
    return out.reshape(x.shape)


def kernel(x, norm_w, w_in, b_gates, q_norm_w, k_norm_w, m_norm_w, w_out):
    tables = _rope_tables(x.shape[1]) + _rotate_half_matrix()
    for layer in range(norm_w.shape[0]):
        x = _layer(x, norm_w[layer], w_in[layer], b_gates[layer], q_norm_w[layer],
                   k_norm_w[layer], m_norm_w[layer], w_out[layer], tables)
    return x
```

```python
import functools
import math

import jax
import jax.numpy as jnp
import numpy as np
from jax import lax
from jax.experimental import pallas as pl
from jax.experimental.pallas import tpu as pltpu

F32 = jnp.float32
BF16 = jnp.bfloat16

D_MODEL = 2048
D_ATT = 1024
ATT_HEADS = 8
HEAD_DIM = 128
ROT_DIM = 32
ROPE_THETA = 500000.0
N_SIDE = 64
M_HEADS = 4
M_QK = 128
M_V = 256
N_GATES = 16
NORM_EPS = 1e-6
NEG = -1e30

LANES = 128
MXU_COLS = 256
MLSTM_CHUNK = 256
PROJ_TM = 1024
PROJ_TN = 1024
PROJ_CHUNKS = (512, 512)
N_MAIN_STEPS = 7
OUT_TM = 512
VMEM_LIMIT = 56 * 1024 * 1024


LOG2_E = math.log2(math.e)


_NT = (((1,), (1,)), ((), ()))


def _proj_kernel(x_ref, nw_ref, w_ref, wqm_ref, wkm_ref, wg_ref, bias_ref, qw_ref, kw_ref,
                 cos_ref, sin_ref, rot_ref,
                 p_ref, kt_ref, gt_ref, xn_ref):
    n = pl.program_id(1)
    tm = x_ref.shape[0]

    @pl.when(n == 0)
    def _():
        x = x_ref[...]
        ms = jnp.mean(x * x, axis=-1, keepdims=True)
        xn_ref[...] = (x * lax.rsqrt(ms + NORM_EPS) * nw_ref[...]).astype(BF16)

    def column_chunks():
        lo = 0
        for width in PROJ_CHUNKS:
            acc = lax.dot_general(xn_ref[...], w_ref[lo:lo + width, :], _NT,
                                  preferred_element_type=F32)
            for i in range(width // MXU_COLS):
                yield (slice(lo + i * MXU_COLS, lo + (i + 1) * MXU_COLS),
                       acc[:, i * MXU_COLS:(i + 1) * MXU_COLS])
            lo += width

    @pl.when(n < 2)
    def _():
        heads = MXU_COLS // HEAD_DIM
        w_row = jnp.where(n == 0, qw_ref[...] * (HEAD_DIM ** -0.5 * LOG2_E), kw_ref[...])
        cos = jnp.concatenate([cos_ref[...]] * heads, axis=1)
        sin = jnp.concatenate([sin_ref[...]] * heads, axis=1)
        for sl, acc in list(column_chunks()):
            sq = acc * acc
            ms = jnp.concatenate(
                [jnp.broadcast_to(jnp.mean(sq[:, h * HEAD_DIM:(h + 1) * HEAD_DIM], axis=1,
                                           keepdims=True), (tm, HEAD_DIM)) for h in range(heads)],
                axis=1)
            b = acc * w_row
            partner = jnp.dot(b.astype(BF16), rot_ref[...], preferred_element_type=F32)
            r = (b * cos + partner * sin) * lax.rsqrt(ms + NORM_EPS)
            p_ref[:, sl] = r.astype(BF16)

    @pl.when((n == 2) | (n == 4))
    def _():
        for sl, acc in column_chunks():
            p_ref[:, sl] = acc.astype(BF16)

    @pl.when((n == 3) | (n >= 5) & (n < N_MAIN_STEPS))
    def _():
        sig_only = n == 5
        for sl, acc in column_chunks():
            sg = 0.5 * jnp.tanh(0.5 * acc) + 0.5
            p_ref[:, sl] = (sg * jnp.where(sig_only, 1.0, acc)).astype(BF16)

    @pl.when(n == N_MAIN_STEPS)
    def _():
        xn = xn_ref[...]
        half = M_HEADS * M_QK
        qm = lax.dot_general(xn, wqm_ref[...], _NT, preferred_element_type=F32)
        p_ref[:, :half] = qm.astype(BF16)
        p_ref[:, half:] = jnp.zeros((tm, PROJ_TN - half), BF16)
        t = lax.dot_general(jnp.concatenate([wkm_ref[...], wg_ref[...]], axis=0), xn, _NT,
                            preferred_element_type=F32)
        for hh in range(M_HEADS):
            for j in range(tm // MLSTM_CHUNK):
                kt_ref[hh, j * M_QK:(j + 1) * M_QK, :] = (
                    t[hh * M_QK:(hh + 1) * M_QK, j * MLSTM_CHUNK:(j + 1) * MLSTM_CHUNK]
                    * (M_QK ** -0.5))
        g = t[half:, :]
        for j in range(tm // LANES):
            gt_ref[j * N_GATES:(j + 1) * N_GATES, :] = g[:, j * LANES:(j + 1) * LANES] + bias_ref[...]


def _main_weight_block(n):
    seg = jnp.minimum(n, N_MAIN_STEPS - 1)
    return seg + (seg >= 4).astype(jnp.int32)


def _input_projection(x2, norm_w, wt_bf, bias_bc, qw, kw, cos_t, sin_t, rot_m, seq):
    n_tok = x2.shape[0]
    tm, tn = PROJ_TM, PROJ_TN
    tiles_per_seq = seq // tm
    const = lambda m, n: (0, 0)
    grid = (n_tok // tm, N_MAIN_STEPS + 1)
    one = pl.Buffered(1)
    qm_rows = M_HEADS * M_QK
    qm_block = 4 * D_ATT // qm_rows
    gate_block = (wt_bf.shape[0] - N_GATES) // N_GATES
    return pl.pallas_call(
        _proj_kernel,
        grid=grid,
        in_specs=[
            pl.BlockSpec((tm, D_MODEL), lambda m, n: (m, 0)),
            pl.BlockSpec((1, D_MODEL), const, pipeline_mode=one),
            pl.BlockSpec((tn, D_MODEL), lambda m, n: (_main_weight_block(n), 0)),
            pl.BlockSpec((qm_rows, D_MODEL), lambda m, n: (qm_block, 0), pipeline_mode=one),
            pl.BlockSpec((qm_rows, D_MODEL), lambda m, n: (qm_block + 1, 0), pipeline_mode=one),
            pl.BlockSpec((N_GATES, D_MODEL), lambda m, n: (gate_block, 0), pipeline_mode=one),
            pl.BlockSpec(bias_bc.shape, const, pipeline_mode=one),
            pl.BlockSpec((1, MXU_COLS), const, pipeline_mode=one),
            pl.BlockSpec((1, MXU_COLS), const, pipeline_mode=one),
            pl.BlockSpec((tm, HEAD_DIM), lambda m, n: (m % tiles_per_seq, 0)),
            pl.BlockSpec((tm, HEAD_DIM), lambda m, n: (m % tiles_per_seq, 0)),
            pl.BlockSpec((MXU_COLS, MXU_COLS), const, pipeline_mode=one),
        ],
        out_specs=[
            pl.BlockSpec((tm, tn), lambda m, n: (m, n)),
            pl.BlockSpec((M_HEADS, tm // MLSTM_CHUNK * M_QK, MLSTM_CHUNK), lambda m, n: (0, m, 0)),
            pl.BlockSpec((tm // LANES * N_GATES, LANES), lambda m, n: (m, 0)),
        ],
        out_shape=[
            jax.ShapeDtypeStruct((n_tok, (N_MAIN_STEPS + 1) * tn), BF16),
            jax.ShapeDtypeStruct((M_HEADS, n_tok // MLSTM_CHUNK * M_QK, MLSTM_CHUNK), F32),
            jax.ShapeDtypeStruct((n_tok // LANES * N_GATES, LANES), F32),
        ],
        scratch_shapes=[pltpu.VMEM((tm, D_MODEL), BF16)],
        compiler_params=pltpu.CompilerParams(
            dimension_semantics=("arbitrary", "arbitrary"), vmem_limit_bytes=VMEM_LIMIT),
        name="in_proj",
    )(x2, norm_w, wt_bf, wt_bf, wt_bf, wt_bf, bias_bc, qw, kw, cos_t, sin_t, rot_m)


ATT_QT = 128
ATT_KT = ATT_QT + 2 * N_SIDE
ATT_GROUP = 32


def _band_bias(nq, nk, off):
    rel = (lax.broadcasted_iota(jnp.int32, (nq, nk), 1)
           - lax.broadcasted_iota(jnp.int32, (nq, nk), 0)) + off
    return jnp.where(jnp.abs(rel) <= N_SIDE, 0.0, NEG).astype(F32)


def _attn_kernel(q_ref, k_ref, v_ref, z_ref, o_ref, nat_f, res4_f, st4, st1, bias_s, bias16_s):
    seq = q_ref.shape[0]
    len4 = seq // 4
    for t, ref in enumerate((q_ref, k_ref, v_ref)):
        nat_f[t] = ref[...].astype(F32)
        for r in range(4):
            res4_f[t, r] = nat_f[t, pl.ds(r, len4, stride=4), :]
    nq16 = seq // 16

    @pl.when((pl.program_id(0) == 0) & (pl.program_id(1) == 0))
    def _():
        for j in range(3):
            bias_s[j] = _band_bias(ATT_QT, ATT_KT, -N_SIDE * j)
        bias16_s[...] = _band_bias(nq16, nq16, 0)

    ones = jnp.ones((ATT_KT, HEAD_DIM), BF16)

    def tile(q, k, v, bias, old):
        s = lax.dot_general(q, k, (((1,), (1,)), ((), ())), preferred_element_type=F32) + bias
        m_t = jnp.max(s, axis=1, keepdims=True)
        if old is None:
            m_n = jnp.broadcast_to(m_t, (q.shape[0], HEAD_DIM))
        else:
            m_o, l_o, a_o = old
            m_n = jnp.maximum(m_o, m_t)
        p = jnp.exp2(s - jnp.concatenate([m_n] * (s.shape[1] // HEAD_DIM), axis=1)).astype(BF16)
        r = jnp.dot(p, jnp.concatenate([v, ones], axis=1), preferred_element_type=F32)
        a_t, l_t = r[:, :HEAD_DIM], r[:, HEAD_DIM:]
        if old is None:
            return m_n, l_t, a_t
        alpha = jnp.exp2(m_o - m_n)
        return m_n, alpha * l_o + l_t, alpha * a_o + a_t

    def tile_geometry(i, length):
        kl = jnp.clip(ATT_QT * i - N_SIDE, 0, length - ATT_KT)
        return (pl.ds(pl.multiple_of(ATT_QT * i, ATT_QT), ATT_QT),
                pl.ds(pl.multiple_of(kl, N_SIDE), ATT_KT),
                (ATT_QT * i - kl) // N_SIDE)

    per_iter16 = ATT_GROUP // 8

    def body16(jg, carry):
        for jj in range(per_iter16):
            rows = pl.ds(jg * per_iter16 + jj, nq16, stride=4)
            for r in range(4):
                q, k, v = (res4_f[t, r, rows, :].astype(BF16) for t in range(3))
                state = tile(q, k, v, bias16_s[...], None)
                for t in range(3):
                    st4[t, r, rows, :] = state[t]
        return carry

    lax.fori_loop(0, 4 // per_iter16, body16, 0)

    def body4(ig, carry):
        for ii in range(ATT_GROUP // 4):
            i = ig * (ATT_GROUP // 4) + ii
            qs, ks, bi = tile_geometry(i, len4)
            bias = bias_s[bi]
            for r in range(4):
                state = tile(res4_f[0, r, qs, :].astype(BF16), res4_f[1, r, ks, :].astype(BF16),
                             res4_f[2, r, ks, :].astype(BF16), bias,
                             tuple(st4[t, r, qs, :] for t in range(3)))
                tok = pl.ds(r + 4 * ATT_QT * i, ATT_QT, stride=4)
                for t in range(3):
                    st1[t, tok, :] = state[t]
        return carry

    lax.fori_loop(0, len4 // ATT_QT // (ATT_GROUP // 4), body4, 0)

    def body1(ig, carry):
        for ii in range(ATT_GROUP):
            qs, ks, bi = tile_geometry(ig * ATT_GROUP + ii, seq)
            _, l, a = tile(q_ref[qs, :], k_ref[ks, :], v_ref[ks, :], bias_s[bi],
                           tuple(st1[t, qs, :] for t in range(3)))
            o_ref[qs, :] = (a / l * z_ref[qs, :].astype(F32)).astype(BF16)
        return carry

    lax.fori_loop(0, seq // ATT_QT // ATT_GROUP, body1, 0)


def _attention(p, batch, seq):
    blk = lambda col0: pl.BlockSpec((seq, HEAD_DIM), lambda b, h: (b, col0 + h))
    return pl.pallas_call(
        _attn_kernel,
        grid=(batch, ATT_HEADS),
        in_specs=[blk(0), blk(ATT_HEADS), blk(2 * ATT_HEADS), blk(3 * ATT_HEADS)],
        out_specs=pl.BlockSpec((seq, HEAD_DIM), lambda b, h: (b, h)),
        out_shape=jax.ShapeDtypeStruct((batch * seq, D_ATT), BF16),
        scratch_shapes=[
            pltpu.VMEM((3, seq, HEAD_DIM), F32),
            pltpu.VMEM((3, 4, seq // 4, HEAD_DIM), F32),
            pltpu.VMEM((3, 4, seq // 4, HEAD_DIM), F32),
            pltpu.VMEM((3, seq, HEAD_DIM), F32),
            pltpu.VMEM((3, ATT_QT, ATT_KT), F32),
            pltpu.VMEM((seq // 16, seq // 16), F32)],
        compiler_params=pltpu.CompilerParams(
            dimension_semantics=("arbitrary", "arbitrary"), vmem_limit_bytes=VMEM_LIMIT),
        name="dilated_attn",
    )(p, p, p, p)


def _scan_lanes(x, op, identity, reverse):
    width = x.shape[1]
    pos = lax.broadcasted_iota(jnp.int32, x.shape, 1)
    k = 1
    while k < width:
        if reverse:
            shifted = jnp.where(pos < width - k, pltpu.roll(x, width - k, 1), identity)
        else:
            shifted = jnp.where(pos >= k, pltpu.roll(x, k, 1), identity)
        x = op(x, shifted)
        k *= 2
    return x


def _mlstm_kernel(q_ref, kt_ref, v_ref, og_ref, zg_ref, g_ref, nw_ref, y_ref,
                  hbuf, c_ref, u_s, a_s, sig_s, mp_s, al_s, causal_s):
    head = pl.program_id(1)
    seq = q_ref.shape[0]
    tc = MLSTM_CHUNK
    nchunk = seq // tc
    blocks_per_chunk = tc // LANES

    def gate_rows(g):
        halves = [g_ref[pl.ds(hf * N_GATES + g, nchunk, stride=blocks_per_chunk * N_GATES), :]
                  for hf in range(blocks_per_chunk)]
        return jnp.concatenate(halves, axis=1)

    ones = jnp.ones((tc, LANES), BF16)

    def prepare(d, h):
        reverse = d == 1
        ch = d * M_HEADS + h
        li = gate_rows(2 * d * M_HEADS + h) * LOG2_E
        fp = gate_rows((2 * d + 1) * M_HEADS + h)
        lf = (jnp.minimum(fp, 0.0) - jnp.log1p(jnp.exp(-jnp.abs(fp)))) * LOG2_E
        b = _scan_lanes(lf, jnp.add, 0.0, reverse)
        u = li - b
        um = _scan_lanes(u, jnp.maximum, NEG, reverse)
        u_s[ch] = u
        m_prev = jnp.full((1, tc), NEG, F32)
        order = range(nchunk - 1, -1, -1) if reverse else range(nchunk)
        for c in order:
            u_max = jnp.max(u[c:c + 1, :], axis=1, keepdims=True)
            b_end = jnp.sum(lf[c:c + 1, :], axis=1, keepdims=True)
            a_end = jnp.maximum(m_prev, u_max)
            mp_s[ch, c:c + 1, :] = m_prev
            al_s[ch, c:c + 1, :] = a_end
            m_prev = b_end + a_end
        a = jnp.maximum(mp_s[ch], um)
        a_s[ch] = a
        sig_s[ch] = b + a

    @pl.when(head == 0)
    def _():
        for d in range(2):
            for h in range(M_HEADS):
                prepare(d, h)
        row_i = lax.broadcasted_iota(jnp.int32, (tc, tc), 0)
        col_i = lax.broadcasted_iota(jnp.int32, (tc, tc), 1)
        causal_s[0] = jnp.where(col_i <= row_i, 0.0, NEG).astype(F32)
        causal_s[1] = jnp.where(col_i >= row_i, 0.0, NEG).astype(F32)

    def to_col(row):
        return jnp.transpose(jnp.broadcast_to(row, (LANES, tc)))

    def chunk(d, c):
        reverse = d == 1
        tok = pl.ds(pl.multiple_of(c * tc, tc), tc)
        q = q_ref[tok, :]
        kt = kt_ref[pl.ds(pl.multiple_of(c * M_QK, M_QK), M_QK), :]
        v = v_ref[tok, :]
        va = jnp.concatenate([v, ones], axis=1)
        ch = d * M_HEADS + head
        u = u_s[ch, pl.ds(c, 1), :]
        a_col = to_col(a_s[ch, pl.ds(c, 1), :])
        sig_col = to_col(sig_s[ch, pl.ds(c, 1), :])
        m_prev = mp_s[ch, pl.ds(c, 1), :]
        a_end = al_s[ch, pl.ds(c, 1), :]

        c_aug = c_ref[d]
        qkc = jnp.dot(q, jnp.concatenate([kt.astype(BF16), c_aug.astype(BF16)], axis=1),
                      preferred_element_type=F32)
        s, inter = qkc[:, :tc], qkc[:, tc:]
        arg = u - jnp.concatenate([a_col] * (tc // LANES), axis=1)
        sw = s * jnp.exp2(arg + causal_s[d])
        intra = jnp.dot(sw.astype(BF16), v, preferred_element_type=F32)
        w_inter = jnp.exp2(m_prev[:, :LANES] - a_col)
        num = jnp.concatenate([w_inter] * (M_V // LANES), axis=1) * inter[:, :M_V] + intra
        den = w_inter * inter[:, M_V:] + jnp.sum(sw, axis=1, keepdims=True)
        inv = 1.0 / jnp.maximum(jnp.abs(den), jnp.exp2(-sig_col))
        h = num * jnp.concatenate([inv] * (M_V // LANES), axis=1)
        wk = jnp.exp2(u - a_end)
        upd = jnp.dot((kt * wk).astype(BF16), va, preferred_element_type=F32)
        decay = jnp.exp2(m_prev[:, :LANES] - a_end[:, :LANES])
        c_ref[d] = jnp.concatenate([decay] * (c_aug.shape[1] // LANES), axis=1) * c_aug + upd
        return h

    c_ref[...] = jnp.zeros(c_ref.shape, F32)

    def finalize(c, h):
        tok = pl.ds(pl.multiple_of(c * tc, tc), tc)
        hs = hbuf[tok, :] + h
        ms = jnp.mean(hs * hs, axis=-1, keepdims=True)
        y = hs * lax.rsqrt(ms + NORM_EPS) * nw_ref[...]
        y = og_ref[tok, :].astype(F32) * y * zg_ref[tok, :].astype(F32)
        y_ref[tok, :] = y.astype(BF16)

    def first_half(i, carry):
        cb = nchunk - 1 - i
        hbuf[pl.ds(pl.multiple_of(i * tc, tc), tc), :] = chunk(0, i)
        hbuf[pl.ds(pl.multiple_of(cb * tc, tc), tc), :] = chunk(1, cb)
        return carry

    def second_half(i, carry):
        cb = nchunk - 1 - i
        finalize(i, chunk(0, i))
        finalize(cb, chunk(1, cb))
        return carry

    lax.fori_loop(0, nchunk // 2, first_half, 0, unroll=4)
    lax.fori_loop(nchunk // 2, nchunk, second_half, 0, unroll=4)


def _mlstm(p, kt, gt, m_norm_w, batch, seq):
    nchunk = seq // MLSTM_CHUNK
    qm_col0 = N_MAIN_STEPS * PROJ_TN // M_QK
    vblk = lambda seg: pl.BlockSpec((seq, M_V), lambda b, h: (b, seg * (PROJ_TN // M_V) + h))
    rows = pltpu.VMEM((2 * M_HEADS, nchunk, MLSTM_CHUNK), F32)
    return pl.pallas_call(
        _mlstm_kernel,
        grid=(batch, M_HEADS),
        in_specs=[
            pl.BlockSpec((seq, M_QK), lambda b, h: (b, qm_col0 + h)),
            pl.BlockSpec((None, nchunk * M_QK, MLSTM_CHUNK), lambda b, h: (h, b, 0)),
            vblk(4), vblk(5), vblk(6),
            pl.BlockSpec((seq // LANES * N_GATES, LANES), lambda b, h: (b, 0)),
            pl.BlockSpec((1, M_V), lambda b, h: (0, h)),
        ],
        out_specs=pl.BlockSpec((seq, M_V), lambda b, h: (b, h)),
        out_shape=jax.ShapeDtypeStruct((batch * seq, M_HEADS * M_V), BF16),
        scratch_shapes=[
            pltpu.VMEM((seq, M_V), F32),
            pltpu.VMEM((2, M_QK, M_V + LANES), F32),
            rows, rows, rows, rows, rows,
            pltpu.VMEM((2, MLSTM_CHUNK, MLSTM_CHUNK), F32),
        ],
        compiler_params=pltpu.CompilerParams(
            dimension_semantics=("arbitrary", "arbitrary"), vmem_limit_bytes=VMEM_LIMIT),
        name="mlstm",
    )(p, kt, p, p, p, gt, m_norm_w)


def _out_kernel(x_ref, ya_ref, ym_ref, w_ref, o_ref, wb_ref):
    @pl.when(pl.program_id(0) == 0)
    def _():
        wb_ref[...] = w_ref[...].astype(BF16)

    d_a = ya_ref.shape[1]
    acc = jnp.dot(ya_ref[...], wb_ref[:d_a, :], preferred_element_type=F32)
    acc = acc + jnp.dot(ym_ref[...], wb_ref[d_a:, :], preferred_element_type=F32)
    o_ref[...] = x_ref[...] + acc


def _output_projection(x2, ya, ym, w_out):
    n_tok = x2.shape[0]
    tm = OUT_TM
    return pl.pallas_call(
        _out_kernel,
        grid=(n_tok // tm,),
        in_specs=[
            pl.BlockSpec((tm, D_MODEL), lambda m: (m, 0)),
            pl.BlockSpec((tm, ya.shape[1]), lambda m: (m, 0)),
            pl.BlockSpec((tm, ym.shape[1]), lambda m: (m, 0)),
            pl.BlockSpec(w_out.shape, lambda m: (0, 0), pipeline_mode=pl.Buffered(1)),
        ],
        out_specs=pl.BlockSpec((tm, D_MODEL), lambda m: (m, 0)),
        out_shape=jax.ShapeDtypeStruct((n_tok, D_MODEL), F32),
        scratch_shapes=[pltpu.VMEM(w_out.shape, BF16)],
        compiler_params=pltpu.CompilerParams(
            dimension_semantics=("arbitrary",), vmem_limit_bytes=VMEM_LIMIT),
        name="out_proj",
    )(x2, ya, ym, w_out)


def _rope_tables(seq):
    pos = np.arange(seq, dtype=np.float64)
    inv_freq = ROPE_THETA ** (-np.arange(0, ROT_DIM, 2, dtype=np.float64) / ROT_DIM)
    lane = np.arange(HEAD_DIM)
    freq = np.where(lane < ROT_DIM, inv_freq[lane % (ROT_DIM // 2)], 0.0)
    ang = pos[:, None] * freq[None, :]
    return jnp.asarray(np.cos(ang), F32), jnp.asarray(np.sin(ang), F32)


def _rotate_half_matrix():
    half = ROT_DIM // 2
    src = np.arange(MXU_COLS)[:, None]
    dst = np.arange(MXU_COLS)[None, :]
    same_head = (src // HEAD_DIM) == (dst // HEAD_DIM)
    d = dst % HEAD_DIM
    rot = (np.where(same_head & (d < half) & (src == dst + half), -1.0, 0.0)
           + np.where(same_head & (d >= half) & (d < ROT_DIM) & (src == dst - half), 1.0, 0.0))
    return (jnp.asarray(rot, BF16),)


def _layer(x, norm_w, w_in, b_gates, q_norm_w, k_norm_w, m_norm_w, w_out, tables):
    batch, seq, _ = x.shape
    x2 = x.reshape(batch * seq, D_MODEL)
    wt_bf = w_in.T.astype(BF16)
    bias_bc = jnp.broadcast_to(b_gates.astype(F32).reshape(N_GATES, 1), (N_GATES, LANES))
    heads = MXU_COLS // HEAD_DIM
    p, kt, gt = _input_projection(
        x2, norm_w.reshape(1, D_MODEL), wt_bf, bias_bc,
        jnp.tile(q_norm_w.reshape(1, HEAD_DIM), (1, heads)),
        jnp.tile(k_norm_w.reshape(1, HEAD_DIM), (1, heads)), *tables, seq)
    ya = _attention(p, batch, seq)
    ym = _mlstm(p, kt, gt, m_norm_w.reshape(1, M_HEADS * M_V), batch, seq)
    out = _output_projection(x2, ya, ym, w_out)
    return out.reshape(x.shape)


def kernel(x, norm_w, w_in, b_gates, q_norm_w, k_norm_w, m_norm_w, w_out):
    tables = _rope_tables(x.shape[1]) + _rotate_half_matrix()
    for layer in range(norm_w.shape[0]):
        x = _layer(x, norm_w[layer], w_in[layer], b_gates[layer], q_norm_w[layer],
                   k_norm_w[layer], m_norm_w[layer], w_out[layer], tables)
    return x
```
